```python
import math
import jax, jax.numpy as jnp
from jax import lax
import numpy as np

D_MODEL = 1024
BATCH = 4
SEQ = 4096
DEPTH = 2
DEC_BATCH = 2
DEC_SEQ = 8192
PAST_LEN = 128

N_EVEN = (DEPTH + 1) // 2
N_ODD = DEPTH // 2
D_FF = 4 * D_MODEL
EPS = 1e-5
A_WIDTH = D_MODEL // 2
A_GROUPS = 4
A_GROUP_DIM = A_WIDTH // A_GROUPS
CHUNK = 128
B_HEADS = 4
B_QK_DIM = 64
B_V_DIM = 2 * B_QK_DIM
B_WIDTH = B_HEADS * B_V_DIM
Q_BLOCK = 128
ROT_DIM = B_QK_DIM // 4
ROPE_THETA = 500000.0
E_IN = 2 * A_WIDTH + 2 * B_HEADS * 2 * B_QK_DIM + B_WIDTH
E_MIX = A_WIDTH + B_WIDTH
CONV_W = 3

kernel_name = "hybrid_gmlp_diffattn_shortconv_encoder"


def _rmsnorm(x, g):
    x32 = x.astype(jnp.float32)
    y = x32 * lax.rsqrt(jnp.mean(x32 * x32, axis=-1, keepdims=True) + EPS)
    return y.astype(x.dtype) * g


def _lambda_init(layer_idx):
    return 0.8 - 0.6 * math.exp(-0.3 * layer_idx)


def _rope_tables(seq_len, dtype):
    inv = ROPE_THETA ** (-jnp.arange(0, ROT_DIM, 2, dtype=jnp.float32) / ROT_DIM)
    ang = jnp.arange(seq_len, dtype=jnp.float32)[:, None] * inv[None, :]
    return jnp.cos(ang).astype(dtype), jnp.sin(ang).astype(dtype)


def _rope(x, cos, sin):
    xr, xp = x[..., :ROT_DIM], x[..., ROT_DIM:]
    half = ROT_DIM // 2
    x1, x2 = xr[..., :half], xr[..., half:]
    c, s = cos[None, :, None, :], sin[None, :, None, :]
    return jnp.concatenate([x1 * c - x2 * s, x2 * c + x1 * s, xp], axis=-1)


def _gmlp_spatial(a_u, a_v, vnorm_g, w_s, b_s):
    bsz, s, _ = a_u.shape
    u = jax.nn.gelu(a_u)
    v = jax.nn.gelu(a_v).reshape(bsz, s, A_GROUPS, A_GROUP_DIM)
    v = _rmsnorm(v, vnorm_g.reshape(A_GROUPS, A_GROUP_DIM))
    v = v.reshape(bsz, s // CHUNK, CHUNK, A_GROUPS, A_GROUP_DIM)
    mixed = jnp.einsum('gpq,bcqge->bcpge', w_s, v) + b_s.T[None, None, :, :, None]
    return u * mixed.reshape(bsz, s, A_WIDTH)


def _diff_attention(q, k, v, lq1, lk1, lq2, lk2, subln_g, layer_idx):
    bsz, s, _ = q.shape
    q = q.reshape(bsz, s, B_HEADS, 2, B_QK_DIM)
    k = k.reshape(bsz, s, B_HEADS, 2, B_QK_DIM)
    v = v.reshape(bsz, s, B_HEADS, B_V_DIM)
    cos, sin = _rope_tables(s, q.dtype)
    q1, q2 = _rope(q[:, :, :, 0], cos, sin), _rope(q[:, :, :, 1], cos, sin)
    k1, k2 = _rope(k[:, :, :, 0], cos, sin), _rope(k[:, :, :, 1], cos, sin)
    lam_init = _lambda_init(layer_idx)
    lam = (jnp.exp(jnp.sum(lq1.astype(jnp.float32) * lk1.astype(jnp.float32)))
           - jnp.exp(jnp.sum(lq2.astype(jnp.float32) * lk2.astype(jnp.float32))) + lam_init)
    scale = 1.0 / math.sqrt(B_QK_DIM)
    nblk = s // Q_BLOCK

    def to_blocks(t):
        return t.reshape(bsz, nblk, Q_BLOCK, B_HEADS, B_QK_DIM).transpose(1, 0, 2, 3, 4)

    def block(args):
        q1b, q2b = args
        s1 = jnp.einsum('bqhd,bkhd->bhqk', q1b, k1).astype(jnp.float32) * scale
        s2 = jnp.einsum('bqhd,bkhd->bhqk', q2b, k2).astype(jnp.float32) * scale
        p = jax.nn.softmax(s1, axis=-1) - lam * jax.nn.softmax(s2, axis=-1)
        return jnp.einsum('bhqk,bkhe->bqhe', p.astype(v.dtype), v)

    o = lax.map(block, (to_blocks(q1), to_blocks(q2)))
    o = o.transpose(1, 0, 2, 3, 4).reshape(bsz, s, B_HEADS, B_V_DIM)
    o = _rmsnorm(o, subln_g) * (1.0 - lam_init)
    return o.reshape(bsz, s, B_WIDTH)


def _short_conv(xn, w_in, conv_w, w_out):
    z = xn @ w_in
    bg, cg, h = jnp.split(z, 3, axis=-1)
    t = cg * h
    tp = jnp.pad(t, ((0, 0), (1, 1), (0, 0)))
    y = tp[:, :-2] * conv_w[0] + tp[:, 1:-1] * conv_w[1] + tp[:, 2:] * conv_w[2]
    return (bg * y) @ w_out


def _ffn(xn, w1, w2):
    return jnp.square(jax.nn.relu(xn @ w1)) @ w2


def _trunk(x, norm_mix_g, norm_ffn_g, ffn_w1, ffn_w2, e_w_in, e_w_out, a_vnorm_g, a_w_s, a_b_s,
           b_lq1, b_lk1, b_lq2, b_lk2, b_subln_g, c_w_in, c_conv_w, c_w_out, final_g):
    h = x
    splits = [A_WIDTH, 2 * A_WIDTH, 2 * A_WIDTH + 2 * B_HEADS * B_QK_DIM,
              2 * A_WIDTH + 4 * B_HEADS * B_QK_DIM]
    for i in range(DEPTH):
        xn = _rmsnorm(h, norm_mix_g[i])
        if i % 2 == 0:
            e = i // 2
            z = xn @ e_w_in[e]
            a_u, a_v, q, k, v = jnp.split(z, splits, axis=-1)
            out_a = _gmlp_spatial(a_u, a_v, a_vnorm_g[e], a_w_s[e], a_b_s[e])
            out_b = _diff_attention(q, k, v, b_lq1[e], b_lk1[e], b_lq2[e], b_lk2[e], b_subln_g[e], i)
            h = h + jnp.concatenate([out_a, out_b], axis=-1) @ e_w_out[e]
        else:
            o = i // 2
            h = h + _short_conv(xn, c_w_in[o], c_conv_w[o], c_w_out[o])
        h = h + _ffn(_rmsnorm(h, norm_ffn_g[i]), ffn_w1[i], ffn_w2[i])
    return _rmsnorm(h, final_g)


def setup_inputs(seed: int = 0) -> dict:
    key = jax.random.key(seed)
    ks = jax.random.split(key, 24)
    f32 = jnp.float32
    n = lambda k, shape, s: jax.random.normal(k, shape, f32) * s
    return {
        "x_prompt": n(ks[0], (BATCH, SEQ, D_MODEL), 1.0),
        "x_sample": n(ks[1], (DEC_BATCH, DEC_SEQ, D_MODEL), 1.0),
        "norm_mix_g": 1.0 + n(ks[2], (DEPTH, D_MODEL), 0.02),
        "norm_ffn_g": 1.0 + n(ks[3], (DEPTH, D_MODEL), 0.02),
        "ffn_w1": n(ks[4], (DEPTH, D_MODEL, D_FF), D_MODEL ** -0.5),
        "ffn_w2": n(ks[5], (DEPTH, D_FF, D_MODEL), 0.5 * D_FF ** -0.5),
        "e_w_in": n(ks[6], (N_EVEN, D_MODEL, E_IN), D_MODEL ** -0.5),
        "e_w_out": n(ks[7], (N_EVEN, E_MIX, D_MODEL), E_MIX ** -0.5),
        "a_vnorm_g": 1.0 + n(ks[8], (N_EVEN, A_WIDTH), 0.02),
        "a_w_s": n(ks[9], (N_EVEN, A_GROUPS, CHUNK, CHUNK), CHUNK ** -0.5),
        "a_b_s": 1.0 + n(ks[10], (N_EVEN, A_GROUPS, CHUNK), 0.02),
        "b_lq1": n(ks[11], (N_EVEN, B_QK_DIM), 0.1),
        "b_lk1": n(ks[12], (N_EVEN, B_QK_DIM), 0.1),
        "b_lq2": n(ks[13], (N_EVEN, B_QK_DIM), 0.1),
        "b_lk2": n(ks[14], (N_EVEN, B_QK_DIM), 0.1),
        "b_subln_g": 1.0 + n(ks[15], (N_EVEN, B_V_DIM), 0.02),
        "c_w_in": n(ks[16], (N_ODD, D_MODEL, 3 * D_MODEL), D_MODEL ** -0.5),
        "c_conv_w": n(ks[17], (N_ODD, CONV_W, D_MODEL), CONV_W ** -0.5),
        "c_w_out": n(ks[18], (N_ODD, D_MODEL, D_MODEL), D_MODEL ** -0.5),
        "final_g": 1.0 + n(ks[19], (D_MODEL,), 0.02),
    }


def reference(x_prompt, x_sample, norm_mix_g, norm_ffn_g, ffn_w1, ffn_w2, e_w_in, e_w_out,
              a_vnorm_g, a_w_s, a_b_s, b_lq1, b_lk1, b_lq2, b_lk2, b_subln_g,
              c_w_in, c_conv_w, c_w_out, final_g):
    y_prompt = _trunk(x_prompt, norm_mix_g, norm_ffn_g, ffn_w1, ffn_w2, e_w_in, e_w_out, a_vnorm_g,
                      a_w_s, a_b_s, b_lq1, b_lk1, b_lq2, b_lk2, b_subln_g, c_w_in, c_conv_w,
                      c_w_out, final_g)
    y_sample = _trunk(x_sample, norm_mix_g, norm_ffn_g, ffn_w1, ffn_w2, e_w_in, e_w_out, a_vnorm_g,
                      a_w_s, a_b_s, b_lq1, b_lk1, b_lq2, b_lk2, b_subln_g, c_w_in, c_conv_w,
                      c_w_out, final_g)
    return (y_prompt, y_sample)
```

```python
import functools
import math

import jax
import jax.numpy as jnp
from jax import lax
from jax.experimental import pallas as pl
from jax.experimental.pallas import tpu as pltpu

D_MODEL = 1024
D_FF = 4 * D_MODEL
EPS = 1e-5
A_WIDTH = 512
A_GROUPS = 4
A_GROUP_DIM = 128
CHUNK = 128
B_HEADS = 4
B_QK_DIM = 64
B_V_DIM = 128
B_WIDTH = 512
ROT_DIM = 16
ROPE_THETA = 500000.0
E_IN = 2 * A_WIDTH + 2 * B_WIDTH + B_WIDTH
CONV_W = 3

LANES = 128
ROW_TILE = 512
KV_TILE = 512
Q_TILE = 512
HALO = 16
FF_CHUNK = 1024
VMEM_LIMIT = 56 * 1024 * 1024

F32 = jnp.float32
BF16 = jnp.bfloat16


def _resident(shape):
    return pl.BlockSpec(shape, lambda *_: (0,) * len(shape), pipeline_mode=pl.Buffered(1))


def _rms(x, g):
    ms = jnp.mean(x * x, axis=-1, keepdims=True)
    return (x * lax.rsqrt(ms + EPS)) * g


def _dot(a, b):
    return jnp.dot(a, b, preferred_element_type=F32)


def _mix0_in_kernel(x_ref, g_ref, w_ref, vg_ref, ws_ref, bias_ref, cos_ref, sin_ref,
                    oa_ref, q_ref, k_ref, vt_ref):
    xn = _rms(x_ref[...], g_ref[...]).astype(BF16)

    u = jax.nn.gelu(_dot(xn, w_ref[:, 0:A_WIDTH]))
    v = jax.nn.gelu(_dot(xn, w_ref[:, A_WIDTH:2 * A_WIDTH]))
    for g in range(A_GROUPS):
        cols = slice(g * A_GROUP_DIM, (g + 1) * A_GROUP_DIM)
        vn = _rms(v[:, cols], vg_ref[:, cols]).astype(BF16)
        for c in range(ROW_TILE // CHUNK):
            rows = slice(c * CHUNK, (c + 1) * CHUNK)
            mixed = _dot(ws_ref[g], vn[rows, :]) + bias_ref[:, cols]
            oa_ref[rows, cols] = (u[rows, cols] * mixed).astype(BF16)

    lane = lax.broadcasted_iota(jnp.int32, (ROW_TILE, LANES), 1)
    upper_half = (lane % B_QK_DIM) >= ROT_DIM // 2
    cos = cos_ref[...]
    sin = sin_ref[...]

    def rope(t):
        partner = jnp.where(upper_half, pltpu.roll(t, ROT_DIM // 2, 1),
                            pltpu.roll(t, LANES - ROT_DIM // 2, 1))
        return t * cos + partner * sin

    q = _dot(xn, w_ref[:, 2 * A_WIDTH:2 * A_WIDTH + B_WIDTH])
    k = _dot(xn, w_ref[:, 2 * A_WIDTH + B_WIDTH:2 * A_WIDTH + 2 * B_WIDTH])
    scale = 1.0 / math.sqrt(B_QK_DIM)
    for h in range(B_HEADS):
        cols = slice(h * LANES, (h + 1) * LANES)
        q_ref[:, cols] = (rope(q[:, cols]) * scale).astype(BF16)
        k_ref[:, cols] = rope(k[:, cols]).astype(BF16)

    vt = _dot(xn, w_ref[:, 2 * A_WIDTH + 2 * B_WIDTH:E_IN]).T
    for h in range(B_HEADS):
        vt_ref[0, h, 0] = vt[h * B_V_DIM:(h + 1) * B_V_DIM, :].astype(BF16)


def _mix0_in(x2d, bsz, seq, g, w, vg, ws, bias, cos_t, sin_t):
    n = bsz * seq
    tiles_per_seq = seq // ROW_TILE
    row = lambda i: (i, 0)
    return pl.pallas_call(
        _mix0_in_kernel,
        grid=(n // ROW_TILE,),
        in_specs=[
            pl.BlockSpec((ROW_TILE, D_MODEL), row),
            _resident((1, D_MODEL)),
            _resident((D_MODEL, E_IN)),
            _resident((1, A_WIDTH)),
            _resident((A_GROUPS, CHUNK, CHUNK)),
            _resident((CHUNK, A_WIDTH)),
            pl.BlockSpec((ROW_TILE, LANES), lambda i: (i % tiles_per_seq, 0)),
            pl.BlockSpec((ROW_TILE, LANES), lambda i: (i % tiles_per_seq, 0)),
        ],
        out_specs=[
            pl.BlockSpec((ROW_TILE, A_WIDTH), row),
            pl.BlockSpec((ROW_TILE, B_WIDTH), row),
            pl.BlockSpec((ROW_TILE, B_WIDTH), row),
            pl.BlockSpec((1, B_HEADS, 1, B_V_DIM, KV_TILE),
                         lambda i: (i // tiles_per_seq, 0, i % tiles_per_seq, 0, 0)),
        ],
        out_shape=[
            jax.ShapeDtypeStruct((n, A_WIDTH), BF16),
            jax.ShapeDtypeStruct((n, B_WIDTH), BF16),
            jax.ShapeDtypeStruct((n, B_WIDTH), BF16),
            jax.ShapeDtypeStruct((bsz, B_HEADS, seq // KV_TILE, B_V_DIM, KV_TILE), BF16),
        ],
        compiler_params=pltpu.CompilerParams(
            dimension_semantics=("arbitrary",), vmem_limit_bytes=VMEM_LIMIT),
        name="mix0_in",
    )(x2d, g, w, vg, ws, bias, cos_t, sin_t)


def _diff_attn_kernel(lam_init, n_kv, q_ref, k_ref, vt_ref, lq1_ref, lk1_ref, lq2_ref, lk2_ref,
                      sg_ref, o_ref, m_ref, l_ref, acc_ref):
    q = q_ref[...]
    lane = lax.broadcasted_iota(jnp.int32, q.shape, 1)
    zero = jnp.zeros_like(q)
    qs = (jnp.where(lane < B_QK_DIM, q, zero), jnp.where(lane >= B_QK_DIM, q, zero))

    m_ref[...] = jnp.full(m_ref.shape, -jnp.inf, F32)
    l_ref[...] = jnp.zeros(l_ref.shape, F32)
    acc_ref[...] = jnp.zeros(acc_ref.shape, F32)

    def body(j, carry):
        kb = k_ref[pl.ds(pl.multiple_of(j * KV_TILE, KV_TILE), KV_TILE), :]
        vtb = vt_ref[0, 0, j]
        for s in range(2):
            st = lax.dot_general(kb, qs[s], (((1,), (1,)), ((), ())),
                                 preferred_element_type=F32)
            m_old = m_ref[s]
            m_new = jnp.maximum(m_old, jnp.max(st, axis=0, keepdims=True))
            alpha = jnp.exp(m_old - m_new)
            p = jnp.exp(st - m_new)
            l_ref[s] = alpha * l_ref[s] + jnp.sum(p, axis=0, keepdims=True)
            acc_ref[s] = alpha * acc_ref[s] + _dot(vtb, p.astype(BF16))
            m_ref[s] = m_new
        return carry

    lax.fori_loop(0, n_kv, body, 0)

    lam = (jnp.exp(jnp.sum(lq1_ref[...] * lk1_ref[...], axis=-1, keepdims=True))
           - jnp.exp(jnp.sum(lq2_ref[...] * lk2_ref[...], axis=-1, keepdims=True)) + lam_init)
    ot = acc_ref[0] / l_ref[0] - lam * (acc_ref[1] / l_ref[1])
    o = ot.T
    o_ref[...] = (_rms(o, sg_ref[...]) * (1.0 - lam_init)).astype(BF16)


def _diff_attn(q, k, vt, lq1, lk1, lq2, lk2, sg, bsz, seq, lam_init):
    n = bsz * seq
    q_tiles = seq // Q_TILE
    n_kv = seq // KV_TILE
    vec = pl.BlockSpec((1, B_QK_DIM), lambda b, h, i: (0, 0))
    return pl.pallas_call(
        functools.partial(_diff_attn_kernel, lam_init, n_kv),
        grid=(bsz, B_HEADS, q_tiles),
        in_specs=[
            pl.BlockSpec((Q_TILE, LANES), lambda b, h, i: (b * q_tiles + i, h)),
            pl.BlockSpec((seq, LANES), lambda b, h, i: (b, h)),
            pl.BlockSpec((1, 1, n_kv, B_V_DIM, KV_TILE), lambda b, h, i: (b, h, 0, 0, 0)),
            vec, vec, vec, vec,
            pl.BlockSpec((1, B_V_DIM), lambda b, h, i: (0, 0)),
        ],
        out_specs=pl.BlockSpec((Q_TILE, LANES), lambda b, h, i: (b * q_tiles + i, h)),
        out_shape=jax.ShapeDtypeStruct((n, B_WIDTH), BF16),
        scratch_shapes=[
            pltpu.VMEM((2, 1, Q_TILE), F32),
            pltpu.VMEM((2, 1, Q_TILE), F32),
            pltpu.VMEM((2, B_V_DIM, Q_TILE), F32),
        ],
        compiler_params=pltpu.CompilerParams(
            dimension_semantics=("arbitrary", "arbitrary", "arbitrary"),
            vmem_limit_bytes=VMEM_LIMIT),
        name="diff_attn",
    )(q, k, vt, lq1, lk1, lq2, lk2, sg)


def _proj_ffn_kernel(final, h_ref, ma_ref, mb_ref, wo_ref, g_ref, w1_ref, w2_ref, fg_ref, o_ref):
    half = D_MODEL // 2
    h1 = h_ref[...] + _dot(ma_ref[...], wo_ref[0:half, :]) + _dot(mb_ref[...], wo_ref[half:, :])
    xn = _rms(h1, g_ref[...]).astype(BF16)
    ff = None
    for c in range(D_FF // FF_CHUNK):
        cols = slice(c * FF_CHUNK, (c + 1) * FF_CHUNK)
        t = jnp.square(jnp.maximum(_dot(xn, w1_ref[:, cols]), 0.0)).astype(BF16)
        d = _dot(t, w2_ref[cols, :])
        ff = d if ff is None else ff + d
    out = h1 + ff
    if final:
        out = _rms(out, fg_ref[...])
    o_ref[...] = out


def _proj_ffn(h, mix_a, mix_b, b_col, wo, g, w1, w2, fg, final):
    n = h.shape[0]
    half = D_MODEL // 2
    row = lambda i: (i, 0)
    return pl.pallas_call(
        functools.partial(_proj_ffn_kernel, final),
        grid=(n // ROW_TILE,),
        in_specs=[
            pl.BlockSpec((ROW_TILE, D_MODEL), row),
            pl.BlockSpec((ROW_TILE, half), row),
            pl.BlockSpec((ROW_TILE, half), lambda i: (i, b_col)),
            _resident((D_MODEL, D_MODEL)),
            _resident((1, D_MODEL)),
            _resident((D_MODEL, D_FF)),
            _resident((D_FF, D_MODEL)),
            _resident((1, D_MODEL)),
        ],
        out_specs=pl.BlockSpec((ROW_TILE, D_MODEL), row),
        out_shape=jax.ShapeDtypeStruct((n, D_MODEL), F32),
        compiler_params=pltpu.CompilerParams(
            dimension_semantics=("arbitrary",), vmem_limit_bytes=VMEM_LIMIT),
        name="proj_ffn_final" if final else "proj_ffn",
    )(h, mix_a, mix_b, wo, g, w1, w2, fg)


def _conv_in_kernel(tiles_per_seq, x_ref, xp_ref, xn_ref, g_ref, w_ref, cw_ref, o_ref):
    i = pl.program_id(0)
    xe = jnp.concatenate([x_ref[...], xp_ref[...], xn_ref[...]], axis=0)
    xn = _rms(xe, g_ref[...]).astype(BF16)
    bg = _dot(xn[:ROW_TILE], w_ref[:, 0:D_MODEL])
    t = _dot(xn, w_ref[:, D_MODEL:2 * D_MODEL]) * _dot(xn, w_ref[:, 2 * D_MODEL:])
    tm = t[:ROW_TILE]
    before = jnp.where(i % tiles_per_seq == 0, 0.0, t[ROW_TILE + HALO - 1:ROW_TILE + HALO])
    after = jnp.where(i % tiles_per_seq == tiles_per_seq - 1, 0.0,
                      t[ROW_TILE + HALO:ROW_TILE + HALO + 1])
    r = lax.broadcasted_iota(jnp.int32, tm.shape, 0)
    t_prev = jnp.where(r == 0, before, pltpu.roll(tm, 1, 0))
    t_next = jnp.where(r == ROW_TILE - 1, after, pltpu.roll(tm, ROW_TILE - 1, 0))
    y = t_prev * cw_ref[0:1, :] + tm * cw_ref[1:2, :] + t_next * cw_ref[2:3, :]
    o_ref[...] = (bg * y).astype(BF16)


def _conv_in(h, seq, g, w, cw):
    n = h.shape[0]
    tiles_per_seq = seq // ROW_TILE
    halos_per_tile = ROW_TILE // HALO
    last_halo = n // HALO - 1
    return pl.pallas_call(
        functools.partial(_conv_in_kernel, tiles_per_seq),
        grid=(n // ROW_TILE,),
        in_specs=[
            pl.BlockSpec((ROW_TILE, D_MODEL), lambda i: (i, 0)),
            pl.BlockSpec((HALO, D_MODEL), lambda i: (jnp.maximum(i * halos_per_tile - 1, 0), 0)),
            pl.BlockSpec((HALO, D_MODEL),
                         lambda i: (jnp.minimum((i + 1) * halos_per_tile, last_halo), 0)),
            _resident((1, D_MODEL)),
            _resident((D_MODEL, 3 * D_MODEL)),
            _resident((CONV_W, D_MODEL)),
        ],
        out_specs=pl.BlockSpec((ROW_TILE, D_MODEL), lambda i: (i, 0)),
        out_shape=jax.ShapeDtypeStruct((n, D_MODEL), BF16),
        compiler_params=pltpu.CompilerParams(
            dimension_semantics=("arbitrary",), vmem_limit_bytes=VMEM_LIMIT),
        name="conv_in",
    )(h, h, h, g, w, cw)


def _rope_tables(seq):
    inv = ROPE_THETA ** (-jnp.arange(0, ROT_DIM, 2, dtype=F32) / ROT_DIM)
    ang = jnp.arange(seq, dtype=F32)[:, None] * inv[None, :]
    cos, sin = jnp.cos(ang), jnp.sin(ang)
    ones = jnp.ones((seq, B_QK_DIM - ROT_DIM), F32)
    cos64 = jnp.concatenate([cos, cos, ones], axis=1)
    sin64 = jnp.concatenate([-sin, sin, jnp.zeros_like(ones)], axis=1)
    return jnp.concatenate([cos64, cos64], axis=1), jnp.concatenate([sin64, sin64], axis=1)


def _trunk(x, p):
    bsz, seq, _ = x.shape
    h = x.reshape(bsz * seq, D_MODEL)
    cos_t, sin_t = _rope_tables(seq)

    out_a, q, k, vt = _mix0_in(h, bsz, seq, p["mix_g"][0], p["e_w_in"], p["a_vnorm_g"], p["a_w_s"],
                               p["a_bias"], cos_t, sin_t)
    lam_init = 0.8 - 0.6 * math.exp(-0.3 * 0)
    out_b = _diff_attn(q, k, vt, p["b_lq1"], p["b_lk1"], p["b_lq2"], p["b_lk2"], p["b_subln_g"],
                       bsz, seq, lam_init)
    h = _proj_ffn(h, out_a, out_b, 0, p["e_w_out"], p["ffn_g"][0], p["ffn_w1"][0], p["ffn_w2"][0],
                  p["final_g"], final=False)
    mix = _conv_in(h, seq, p["mix_g"][1], p["c_w_in"], p["c_conv_w"])
    h = _proj_ffn(h, mix, mix, 1, p["c_w_out"], p["ffn_g"][1], p["ffn_w1"][1], p["ffn_w2"][1],
                  p["final_g"], final=True)
    return h.reshape(bsz, seq, D_MODEL)


def kernel(x_prompt, x_sample, norm_mix_g, norm_ffn_g, ffn_w1, ffn_w2, e_w_in, e_w_out, a_vnorm_g, a_w_s, a_b_s, b_lq1, b_lk1, b_lq2, b_lk2, b_subln_g, c_w_in, c_conv_w, c_w_out, final_g):
    depth = norm_mix_g.shape[0]
    p = {
        "mix_g": [norm_mix_g[i].reshape(1, D_MODEL) for i in range(depth)],
        "ffn_g": [norm_ffn_g[i].reshape(1, D_MODEL) for i in range(depth)],
        "ffn_w1": [ffn_w1[i].astype(BF16) for i in range(depth)],
        "ffn_w2": [ffn_w2[i].astype(BF16) for i in range(depth)],
        "e_w_in": e_w_in[0].astype(BF16),
        "e_w_out": e_w_out[0].astype(BF16),
        "a_vnorm_g": a_vnorm_g[0].reshape(1, A_WIDTH),
        "a_w_s": a_w_s[0].astype(BF16),
        "a_bias": jnp.repeat(a_b_s[0].T, A_GROUP_DIM, axis=1),
        "b_lq1": b_lq1[0].reshape(1, B_QK_DIM),
        "b_lk1": b_lk1[0].reshape(1, B_QK_DIM),
        "b_lq2": b_lq2[0].reshape(1, B_QK_DIM),
        "b_lk2": b_lk2[0].reshape(1, B_QK_DIM),
        "b_subln_g": b_subln_g[0].reshape(1, B_V_DIM),
        "c_w_in": c_w_in[0].astype(BF16),
        "c_conv_w": c_conv_w[0],
        "c_w_out": c_w_out[0].astype(BF16),
        "final_g": final_g.reshape(1, D_MODEL),
    }
    return (_trunk(x_prompt, p), _trunk(x_sample, p))
```

```python
import functools
import math

import jax
import jax.numpy as jnp
from jax import lax
from jax.experimental import pallas as pl
from jax.experimental.pallas import tpu as pltpu

D_MODEL = 1024
D_FF = 4 * D_MODEL
EPS = 1e-5
A_WIDTH = 512
A_GROUPS = 4
A_GROUP_DIM = 128
CHUNK = 128
B_HEADS = 4
B_QK_DIM = 64
B_V_DIM = 128
B_WIDTH = 512
ROT_DIM = 16
ROPE_THETA = 500000.0
E_IN = 2 * A_WIDTH + 2 * B_WIDTH + B_WIDTH
CONV_W = 3

LANES = 128
ROW_TILE = 512
KV_TILE = 512
Q_TILE = 512
Q_GROUP = 256
KV_STEP = 4
HALO = 16
FF_CHUNK = 1024
VMEM_LIMIT = 56 * 1024 * 1024

F32 = jnp.float32
BF16 = jnp.bfloat16


def _resident(shape):
    return pl.BlockSpec(shape, lambda *_: (0,) * len(shape), pipeline_mode=pl.Buffered(1))


def _rms(x, g):
    ms = jnp.mean(x * x, axis=-1, keepdims=True)
    return (x * lax.rsqrt(ms + EPS)) * g


def _dot(a, b):
    return jnp.dot(a, b, preferred_element_type=F32)


def _reduce_rows(x, op, width=32):
    acc = x[0:width]
    for r in range(width, x.shape[0], width):
        acc = op(acc, x[r:r + width])
    while acc.shape[0] > 8:
        half = acc.shape[0] // 2
        acc = op(acc[:half], acc[half:])
    reduce = jnp.max if op is jnp.maximum else jnp.sum
    return reduce(acc, axis=0, keepdims=True)


def _mix0_in_kernel(x_ref, g_ref, w_ref, vg_ref, ws_ref, bias_ref, cos_ref, sin_ref,
                    oa_ref, q_ref, k_ref, vt_ref):
    xn = _rms(x_ref[...], g_ref[...]).astype(BF16)

    u = jax.nn.gelu(_dot(xn, w_ref[:, 0:A_WIDTH]))
    v = jax.nn.gelu(_dot(xn, w_ref[:, A_WIDTH:2 * A_WIDTH]))
    for g in range(A_GROUPS):
        cols = slice(g * A_GROUP_DIM, (g + 1) * A_GROUP_DIM)
        vn = _rms(v[:, cols], vg_ref[:, cols]).astype(BF16)
        for c in range(ROW_TILE // CHUNK):
            rows = slice(c * CHUNK, (c + 1) * CHUNK)
            mixed = _dot(ws_ref[g], vn[rows, :]) + bias_ref[:, cols]
            oa_ref[rows, cols] = (u[rows, cols] * mixed).astype(BF16)

    lane = lax.broadcasted_iota(jnp.int32, (ROW_TILE, LANES), 1)
    upper_half = (lane % B_QK_DIM) >= ROT_DIM // 2
    cos = cos_ref[...]
    sin = sin_ref[...]

    def rope(t):
        partner = jnp.where(upper_half, pltpu.roll(t, ROT_DIM // 2, 1),
                            pltpu.roll(t, LANES - ROT_DIM // 2, 1))
        return t * cos + partner * sin

    q = _dot(xn, w_ref[:, 2 * A_WIDTH:2 * A_WIDTH + B_WIDTH])
    k = _dot(xn, w_ref[:, 2 * A_WIDTH + B_WIDTH:2 * A_WIDTH + 2 * B_WIDTH])
    scale = math.log2(math.e) / math.sqrt(B_QK_DIM)
    for h in range(B_HEADS):
        cols = slice(h * LANES, (h + 1) * LANES)
        q_ref[:, cols] = (rope(q[:, cols]) * scale).astype(BF16)
        k_ref[:, cols] = rope(k[:, cols]).astype(BF16)

    vt = _dot(xn, w_ref[:, 2 * A_WIDTH + 2 * B_WIDTH:E_IN]).T
    for h in range(B_HEADS):
        vt_ref[0, h, 0] = vt[h * B_V_DIM:(h + 1) * B_V_DIM, :].astype(BF16)


def _mix0_in(x2d, bsz, seq, g, w, vg, ws, bias, cos_t, sin_t):
    n = bsz * seq
    tiles_per_seq = seq // ROW_TILE
    row = lambda i: (i, 0)
    return pl.pallas_call(
        _mix0_in_kernel,
        grid=(n // ROW_TILE,),
        in_specs=[
            pl.BlockSpec((ROW_TILE, D_MODEL), row),
            _resident((1, D_MODEL)),
            _resident((D_MODEL, E_IN)),
            _resident((1, A_WIDTH)),
            _resident((A_GROUPS, CHUNK, CHUNK)),
            _resident((CHUNK, A_WIDTH)),
            pl.BlockSpec((ROW_TILE, LANES), lambda i: (i % tiles_per_seq, 0)),
            pl.BlockSpec((ROW_TILE, LANES), lambda i: (i % tiles_per_seq, 0)),
        ],
        out_specs=[
            pl.BlockSpec((ROW_TILE, A_WIDTH), row),
            pl.BlockSpec((ROW_TILE, B_WIDTH), row),
            pl.BlockSpec((ROW_TILE, B_WIDTH), row),
            pl.BlockSpec((1, B_HEADS, 1, B_V_DIM, KV_TILE),
                         lambda i: (i // tiles_per_seq, 0, i % tiles_per_seq, 0, 0)),
        ],
        out_shape=[
            jax.ShapeDtypeStruct((n, A_WIDTH), BF16),
            jax.ShapeDtypeStruct((n, B_WIDTH), BF16),
            jax.ShapeDtypeStruct((n, B_WIDTH), BF16),
            jax.ShapeDtypeStruct((bsz, B_HEADS, seq // KV_TILE, B_V_DIM, KV_TILE), BF16),
        ],
        compiler_params=pltpu.CompilerParams(
            dimension_semantics=("arbitrary",), vmem_limit_bytes=VMEM_LIMIT),
        name="mix0_in",
    )(x2d, g, w, vg, ws, bias, cos_t, sin_t)


def _diff_attn_kernel(lam_init, n_kv, q_tiles, q_ref, k_ref, vt_ref, lq1_ref, lk1_ref, lq2_ref,
                      lk2_ref, sg_ref, o_ref, m_ref, l_ref, acc_ref, s_ref, mb_ref):
    i = pl.program_id(2)
    units = [(s, c) for s in range(2) for c in range(Q_TILE // Q_GROUP)]
    lane = lax.broadcasted_iota(jnp.int32, (Q_GROUP, LANES), 1)
    stream_lanes = (lane < B_QK_DIM, lane >= B_QK_DIM)

    def stage_scores(slot, qt, jb, u):
        s, c = units[u]
        q_row = pl.multiple_of(qt * Q_TILE + c * Q_GROUP, Q_GROUP)
        qg = q_ref[pl.ds(q_row, Q_GROUP), :]
        qg = jnp.where(stream_lanes[s], qg, jnp.zeros_like(qg))
        kb = k_ref[pl.ds(pl.multiple_of(jb * KV_TILE, KV_TILE), KV_TILE), :]
        st = lax.dot_general(kb, qg, (((1,), (1,)), ((), ())),
                             preferred_element_type=F32)
        s_ref[slot, u] = st
        mb_ref[slot, u] = _reduce_rows(st, jnp.maximum)

    @pl.when(i == 0)
    def _():
        for u in range(len(units)):
            stage_scores(0, 0, 0, u)

    m_ref[...] = jnp.full(m_ref.shape, -jnp.inf, F32)
    l_ref[...] = jnp.zeros(l_ref.shape, F32)
    acc_ref[...] = jnp.zeros(acc_ref.shape, F32)

    n_steps = n_kv // KV_STEP

    def body(t, carry):
        for b in range(KV_STEP):
            jb = t * KV_STEP + b
            slot = b % 2
            vtb = vt_ref[0, 0, jb]
            if b + 1 < KV_STEP:
                qt_next, jb_next = i, jb + 1
            else:
                last = t + 1 == n_steps
                qt_next = jnp.where(last, jnp.minimum(i + 1, q_tiles - 1), i)
                jb_next = jnp.where(last, 0, jb + 1)
            for u, (s, c) in enumerate(units):
                cols = slice(c * Q_GROUP, (c + 1) * Q_GROUP)
                stage_scores(1 - slot, qt_next, jb_next, u)
                m_old = m_ref[s, :, cols]
                m_new = jnp.maximum(m_old, mb_ref[slot, u])
                alpha = jnp.exp2(m_old - m_new)
                p = jnp.exp2(s_ref[slot, u] - m_new)
                l_ref[s, :, cols] = alpha * l_ref[s, :, cols] + _reduce_rows(p, jnp.add)
                acc_ref[s, :, cols] = alpha * acc_ref[s, :, cols] + _dot(vtb, p.astype(BF16))
                m_ref[s, :, cols] = m_new
        return carry

    lax.fori_loop(0, n_steps, body, 0)

    lam = (jnp.exp(jnp.sum(lq1_ref[...] * lk1_ref[...], axis=-1, keepdims=True))
           - jnp.exp(jnp.sum(lq2_ref[...] * lk2_ref[...], axis=-1, keepdims=True)) + lam_init)
    ot = acc_ref[0] / l_ref[0] - lam * (acc_ref[1] / l_ref[1])
    o = ot.T
    o_ref[...] = (_rms(o, sg_ref[...]) * (1.0 - lam_init)).astype(BF16)


def _diff_attn(q, k, vt, lq1, lk1, lq2, lk2, sg, bsz, seq, lam_init):
    n = bsz * seq
    q_tiles = seq // Q_TILE
    n_kv = seq // KV_TILE
    assert n_kv % KV_STEP == 0 and KV_STEP % 2 == 0
    n_units = 2 * Q_TILE // Q_GROUP
    vec = pl.BlockSpec((1, B_QK_DIM), lambda b, h, i: (0, 0))
    return pl.pallas_call(
        functools.partial(_diff_attn_kernel, lam_init, n_kv, q_tiles),
        grid=(bsz, B_HEADS, q_tiles),
        in_specs=[
            pl.BlockSpec((seq, LANES), lambda b, h, i: (b, h)),
            pl.BlockSpec((seq, LANES), lambda b, h, i: (b, h)),
            pl.BlockSpec((1, 1, n_kv, B_V_DIM, KV_TILE), lambda b, h, i: (b, h, 0, 0, 0)),
            vec, vec, vec, vec,
            pl.BlockSpec((1, B_V_DIM), lambda b, h, i: (0, 0)),
        ],
        out_specs=pl.BlockSpec((Q_TILE, LANES), lambda b, h, i: (b * q_tiles + i, h)),
        out_shape=jax.ShapeDtypeStruct((n, B_WIDTH), BF16),
        scratch_shapes=[
            pltpu.VMEM((2, 1, Q_TILE), F32),
            pltpu.VMEM((2, 1, Q_TILE), F32),
            pltpu.VMEM((2, B_V_DIM, Q_TILE), F32),
            pltpu.VMEM((2, n_units, KV_TILE, Q_GROUP), F32),
            pltpu.VMEM((2, n_units, 1, Q_GROUP), F32),
        ],
        compiler_params=pltpu.CompilerParams(
            dimension_semantics=("arbitrary", "arbitrary", "arbitrary"),
            vmem_limit_bytes=VMEM_LIMIT),
        name="diff_attn",
    )(q, k, vt, lq1, lk1, lq2, lk2, sg)


def _proj_ffn_kernel(final, h_ref, ma_ref, mb_ref, wo_ref, g_ref, w1_ref, w2_ref, fg_ref, o_ref):
    half = D_MODEL // 2
    h1 = h_ref[...] + _dot(ma_ref[...], wo_ref[0:half, :]) + _dot(mb_ref[...], wo_ref[half:, :])
    xn = _rms(h1, g_ref[...]).astype(BF16)
    ff = None
    for c in range(D_FF // FF_CHUNK):
        cols = slice(c * FF_CHUNK, (c + 1) * FF_CHUNK)
        t = jnp.square(jnp.maximum(_dot(xn, w1_ref[:, cols]), 0.0)).astype(BF16)
        d = _dot(t, w2_ref[cols, :])
        ff = d if ff is None else ff + d
    out = h1 + ff
    if final:
        out = _rms(out, fg_ref[...])
    o_ref[...] = out


def _proj_ffn(h, mix_a, mix_b, b_col, wo, g, w1, w2, fg, final):
    n = h.shape[0]
    half = D_MODEL // 2
    row = lambda i: (i, 0)
    return pl.pallas_call(
        functools.partial(_proj_ffn_kernel, final),
        grid=(n // ROW_TILE,),
        in_specs=[
            pl.BlockSpec((ROW_TILE, D_MODEL), row),
            pl.BlockSpec((ROW_TILE, half), row),
            pl.BlockSpec((ROW_TILE, half), lambda i: (i, b_col)),
            _resident((D_MODEL, D_MODEL)),
            _resident((1, D_MODEL)),
            _resident((D_MODEL, D_FF)),
            _resident((D_FF, D_MODEL)),
            _resident((1, D_MODEL)),
        ],
        out_specs=pl.BlockSpec((ROW_TILE, D_MODEL), row),
        out_shape=jax.ShapeDtypeStruct((n, D_MODEL), F32),
        compiler_params=pltpu.CompilerParams(
            dimension_semantics=("arbitrary",), vmem_limit_bytes=VMEM_LIMIT),
        name="proj_ffn_final" if final else "proj_ffn",
    )(h, mix_a, mix_b, wo, g, w1, w2, fg)


def _conv_in_kernel(tiles_per_seq, x_ref, xp_ref, xn_ref, g_ref, w_ref, cw_ref, o_ref):
    i = pl.program_id(0)
    xe = jnp.concatenate([x_ref[...], xp_ref[...], xn_ref[...]], axis=0)
    xn = _rms(xe, g_ref[...]).astype(BF16)
    bg = _dot(xn[:ROW_TILE], w_ref[:, 0:D_MODEL])
    t = _dot(xn, w_ref[:, D_MODEL:2 * D_MODEL]) * _dot(xn, w_ref[:, 2 * D_MODEL:])
    tm = t[:ROW_TILE]
    before = jnp.where(i % tiles_per_seq == 0, 0.0, t[ROW_TILE + HALO - 1:ROW_TILE + HALO])
    after = jnp.where(i % tiles_per_seq == tiles_per_seq - 1, 0.0,
                      t[ROW_TILE + HALO:ROW_TILE + HALO + 1])
    r = lax.broadcasted_iota(jnp.int32, tm.shape, 0)
    t_prev = jnp.where(r == 0, before, pltpu.roll(tm, 1, 0))
    t_next = jnp.where(r == ROW_TILE - 1, after, pltpu.roll(tm, ROW_TILE - 1, 0))
    y = t_prev * cw_ref[0:1, :] + tm * cw_ref[1:2, :] + t_next * cw_ref[2:3, :]
    o_ref[...] = (bg * y).astype(BF16)


def _conv_in(h, seq, g, w, cw):
    n = h.shape[0]
    tiles_per_seq = seq // ROW_TILE
    halos_per_tile = ROW_TILE // HALO
    last_halo = n // HALO - 1
    return pl.pallas_call(
        functools.partial(_conv_in_kernel, tiles_per_seq),
        grid=(n // ROW_TILE,),
        in_specs=[
            pl.BlockSpec((ROW_TILE, D_MODEL), lambda i: (i, 0)),
            pl.BlockSpec((HALO, D_MODEL), lambda i: (jnp.maximum(i * halos_per_tile - 1, 0), 0)),
            pl.BlockSpec((HALO, D_MODEL),
                         lambda i: (jnp.minimum((i + 1) * halos_per_tile, last_halo), 0)),
            _resident((1, D_MODEL)),
            _resident((D_MODEL, 3 * D_MODEL)),
            _resident((CONV_W, D_MODEL)),
        ],
        out_specs=pl.BlockSpec((ROW_TILE, D_MODEL), lambda i: (i, 0)),
        out_shape=jax.ShapeDtypeStruct((n, D_MODEL), BF16),
        compiler_params=pltpu.CompilerParams(
            dimension_semantics=("arbitrary",), vmem_limit_bytes=VMEM_LIMIT),
        name="conv_in",
    )(h, h, h, g, w, cw)


def _rope_tables(seq):
    inv = ROPE_THETA ** (-jnp.arange(0, ROT_DIM, 2, dtype=F32) / ROT_DIM)
    ang = jnp.arange(seq, dtype=F32)[:, None] * inv[None, :]
    cos, sin = jnp.cos(ang), jnp.sin(ang)
    ones = jnp.ones((seq, B_QK_DIM - ROT_DIM), F32)
    cos64 = jnp.concatenate([cos, cos, ones], axis=1)
    sin64 = jnp.concatenate([-sin, sin, jnp.zeros_like(ones)], axis=1)
    return jnp.concatenate([cos64, cos64], axis=1), jnp.concatenate([sin64, sin64], axis=1)


def _trunk(x, p):
    bsz, seq, _ = x.shape
    h = x.reshape(bsz * seq, D_MODEL)
    cos_t, sin_t = _rope_tables(seq)

    out_a, q, k, vt = _mix0_in(h, bsz, seq, p["mix_g"][0], p["e_w_in"], p["a_vnorm_g"], p["a_w_s"],
                               p["a_bias"], cos_t, sin_t)
    lam_init = 0.8 - 0.6 * math.exp(-0.3 * 0)
    out_b = _diff_attn(q, k, vt, p["b_lq1"], p["b_lk1"], p["b_lq2"], p["b_lk2"], p["b_subln_g"],
                       bsz, seq, lam_init)
    h = _proj_ffn(h, out_a, out_b, 0, p["e_w_out"], p["ffn_g"][0], p["ffn_w1"][0], p["ffn_w2"][0],
                  p["final_g"], final=False)
    mix = _conv_in(h, seq, p["mix_g"][1], p["c_w_in"], p["c_conv_w"])
    h = _proj_ffn(h, mix, mix, 1, p["c_w_out"], p["ffn_g"][1], p["ffn_w1"][1], p["ffn_w2"][1],
                  p["final_g"], final=True)
    return h.reshape(bsz, seq, D_MODEL)


def kernel(x_prompt, x_sample, norm_mix_g, norm_ffn_g, ffn_w1, ffn_w2, e_w_in, e_w_out, a_vnorm_g, a_w_s, a_b_s, b_lq1, b_lk1, b_lq2, b_lk2, b_subln_g, c_w_in, c_conv_w, c_w_out, final_g):
    depth = norm_mix_g.shape[0]
    p = {
        "mix_g": [norm_mix_g[i].reshape(1, D_MODEL) for i in range(depth)],
        "ffn_g": [norm_ffn_g[i].reshape(1, D_MODEL) for i in range(depth)],
        "ffn_w1": [ffn_w1[i].astype(BF16) for i in range(depth)],
        "ffn_w2": [ffn_w2[i].astype(BF16) for i in range(depth)],
        "e_w_in": e_w_in[0].astype(BF16),
        "e_w_out": e_w_out[0].astype(BF16),
        "a_vnorm_g": a_vnorm_g[0].reshape(1, A_WIDTH),
        "a_w_s": a_w_s[0].astype(BF16),
        "a_bias": jnp.repeat(a_b_s[0].T, A_GROUP_DIM, axis=1),
        "b_lq1": b_lq1[0].reshape(1, B_QK_DIM),
        "b_lk1": b_lk1[0].reshape(1, B_QK_DIM),
        "b_lq2": b_lq2[0].reshape(1, B_QK_DIM),
        "b_lk2": b_lk2[0].reshape(1, B_QK_DIM),
        "b_subln_g": b_subln_g[0].reshape(1, B_V_DIM),
        "c_w_in": c_w_in[0].astype(BF16),
        "c_conv_w": c_conv_w[0],
        "c_w_out": c_w_out[0].astype(BF16),
        "final_g": final_g.reshape(1, D_MODEL),
    }
    return (_trunk(x_prompt, p), _trunk(x_sample, p))
```

```python
import functools
import math

import jax
import jax.numpy as jnp
from jax import lax
from jax.experimental import pallas as pl
from jax.experimental.pallas import tpu as pltpu

D_MODEL = 1024
D_FF = 4 * D_MODEL
EPS = 1e-5
A_WIDTH = 512
A_GROUPS = 4
A_GROUP_DIM = 128
CHUNK = 128
B_HEADS = 4
B_QK_DIM = 64
B_V_DIM = 128
B_WIDTH = 512
ROT_DIM = 16
ROPE_THETA = 500000.0
E_IN = 2 * A_WIDTH + 2 * B_WIDTH + B_WIDTH
CONV_W = 3

LANES = 128
ROW_TILE = 512
SUB_ROWS = 256
KV_TILE = 512
Q_TILE = 1024
Q_GROUP = 256
KV_STEP = 4
HALO = 16
FF_CHUNK = 1024
VMEM_LIMIT = 56 * 1024 * 1024

F32 = jnp.float32
BF16 = jnp.bfloat16


def _resident(shape):
    return pl.BlockSpec(shape, lambda *_: (0,) * len(shape), pipeline_mode=pl.Buffered(1))


def _rms(x, g):
    ms = jnp.mean(x * x, axis=-1, keepdims=True)
    return (x * lax.rsqrt(ms + EPS)) * g


def _dot(a, b):
    return jnp.dot(a, b, preferred_element_type=F32)


def _reduce_rows(x, op, width=32):
    acc = x[0:width]
    for r in range(width, x.shape[0], width):
        acc = op(acc, x[r:r + width])
    while acc.shape[0] > 8:
        half = acc.shape[0] // 2
        acc = op(acc[:half], acc[half:])
    reduce = jnp.max if op is jnp.maximum else jnp.sum
    return reduce(acc, axis=0, keepdims=True)


def _mix0_in_kernel(x_ref, g_ref, w_ref, vg_ref, ws_ref, bias_ref, cos_ref, sin_ref,
                    oa_ref, q_ref, k_ref, vt_ref):
    lane = lax.broadcasted_iota(jnp.int32, (SUB_ROWS, LANES), 1)
    upper_half = (lane % B_QK_DIM) >= ROT_DIM // 2
    scale = math.log2(math.e) / math.sqrt(B_QK_DIM)
    q0, k0, v0 = 2 * A_WIDTH, 2 * A_WIDTH + B_WIDTH, 2 * A_WIDTH + 2 * B_WIDTH

    pieces = []
    for r0 in range(0, ROW_TILE, SUB_ROWS):
        rows = slice(r0, r0 + SUB_ROWS)
        xn = _rms(x_ref[rows, :], g_ref[...]).astype(BF16)
        pieces.append((rows, [_dot(xn, w_ref[:, lo:hi]) for lo, hi in
                              ((A_WIDTH, q0), (0, A_WIDTH), (q0, k0), (k0, v0), (v0, E_IN))]))

    for rows, (av, au, q, k, v) in pieces:
        u = jax.nn.gelu(au)
        gv = jax.nn.gelu(av)
        for g in range(A_GROUPS):
            cols = slice(g * A_GROUP_DIM, (g + 1) * A_GROUP_DIM)
            vn = _rms(gv[:, cols], vg_ref[:, cols]).astype(BF16)
            for c in range(SUB_ROWS // CHUNK):
                sub = slice(c * CHUNK, (c + 1) * CHUNK)
                mixed = _dot(ws_ref[g], vn[sub, :]) + bias_ref[:, cols]
                oa_ref[rows.start + c * CHUNK:rows.start + (c + 1) * CHUNK, cols] = (
                    u[sub, cols] * mixed).astype(BF16)

        cos = cos_ref[rows, :]
        sin = sin_ref[rows, :]

        def rope(t):
            partner = jnp.where(upper_half, pltpu.roll(t, ROT_DIM // 2, 1),
                                pltpu.roll(t, LANES - ROT_DIM // 2, 1))
            return t * cos + partner * sin

        for h in range(B_HEADS):
            cols = slice(h * LANES, (h + 1) * LANES)
            q_ref[rows, cols] = (rope(q[:, cols]) * scale).astype(BF16)
            k_ref[rows, cols] = rope(k[:, cols]).astype(BF16)

        vt = v.T
        for h in range(B_HEADS):
            vt_ref[0, h, 0, :, rows] = vt[h * B_V_DIM:(h + 1) * B_V_DIM, :].astype(BF16)


def _mix0_in(x2d, bsz, seq, g, w, vg, ws, bias, cos_t, sin_t):
    n = bsz * seq
    tiles_per_seq = seq // ROW_TILE
    row = lambda i: (i, 0)
    return pl.pallas_call(
        _mix0_in_kernel,
        grid=(n // ROW_TILE,),
        in_specs=[
            pl.BlockSpec((ROW_TILE, D_MODEL), row),
            _resident((1, D_MODEL)),
            _resident((D_MODEL, E_IN)),
            _resident((1, A_WIDTH)),
            _resident((A_GROUPS, CHUNK, CHUNK)),
            _resident((CHUNK, A_WIDTH)),
            pl.BlockSpec((ROW_TILE, LANES), lambda i: (i % tiles_per_seq, 0)),
            pl.BlockSpec((ROW_TILE, LANES), lambda i: (i % tiles_per_seq, 0)),
        ],
        out_specs=[
            pl.BlockSpec((ROW_TILE, A_WIDTH), row),
            pl.BlockSpec((ROW_TILE, B_WIDTH), row),
            pl.BlockSpec((ROW_TILE, B_WIDTH), row),
            pl.BlockSpec((1, B_HEADS, 1, B_V_DIM, KV_TILE),
                         lambda i: (i // tiles_per_seq, 0, i % tiles_per_seq, 0, 0)),
        ],
        out_shape=[
            jax.ShapeDtypeStruct((n, A_WIDTH), BF16),
            jax.ShapeDtypeStruct((n, B_WIDTH), BF16),
            jax.ShapeDtypeStruct((n, B_WIDTH), BF16),
            jax.ShapeDtypeStruct((bsz, B_HEADS, seq // KV_TILE, B_V_DIM, KV_TILE), BF16),
        ],
        compiler_params=pltpu.CompilerParams(
            dimension_semantics=("arbitrary",), vmem_limit_bytes=VMEM_LIMIT),
        name="mix0_in",
    )(x2d, g, w, vg, ws, bias, cos_t, sin_t)


def _diff_attn_kernel(lam_init, n_kv, q_tiles, q_ref, k_ref, vt_ref, lq1_ref, lk1_ref, lq2_ref,
                      lk2_ref, sg_ref, o_ref, m_ref, l_ref, acc_ref, s_ref, mb_ref):
    i = pl.program_id(2)
    units = [(s, c) for s in range(2) for c in range(Q_TILE // Q_GROUP)]
    lane = lax.broadcasted_iota(jnp.int32, (Q_GROUP, LANES), 1)
    stream_lanes = (lane < B_QK_DIM, lane >= B_QK_DIM)

    def stage_scores(slot, qt, jb, u):
        s, c = units[u]
        q_row = pl.multiple_of(qt * Q_TILE + c * Q_GROUP, Q_GROUP)
        qg = q_ref[pl.ds(q_row, Q_GROUP), :]
        qg = jnp.where(stream_lanes[s], qg, jnp.zeros_like(qg))
        kb = k_ref[pl.ds(pl.multiple_of(jb * KV_TILE, KV_TILE), KV_TILE), :]
        st = lax.dot_general(kb, qg, (((1,), (1,)), ((), ())),
                             preferred_element_type=F32)
        s_ref[slot, u] = st
        mb_ref[slot, u] = _reduce_rows(st, jnp.maximum)

    @pl.when(i == 0)
    def _():
        for u in range(len(units)):
            stage_scores(0, 0, 0, u)

    m_ref[...] = jnp.full(m_ref.shape, -jnp.inf, F32)
    l_ref[...] = jnp.zeros(l_ref.shape, F32)
    acc_ref[...] = jnp.zeros(acc_ref.shape, F32)

    def key_block(jb, slot, prefetch_next):
        vtb = vt_ref[0, 0, jb]
        for u, (s, c) in enumerate(units):
            cols = slice(c * Q_GROUP, (c + 1) * Q_GROUP)
            if prefetch_next:
                stage_scores(1 - slot, i, jb + 1, u)
            m_old = m_ref[s, :, cols]
            m_new = jnp.maximum(m_old, mb_ref[slot, u])
            alpha = jnp.exp2(m_old - m_new)
            p = jnp.exp2(s_ref[slot, u] - m_new)
            l_ref[s, :, cols] = alpha * l_ref[s, :, cols] + _reduce_rows(p, jnp.add)
            acc_ref[s, :, cols] = alpha * acc_ref[s, :, cols] + _dot(vtb, p.astype(BF16))
            m_ref[s, :, cols] = m_new

    def body(t, carry):
        for b in range(KV_STEP):
            key_block(t * KV_STEP + b, b % 2, True)
        return carry

    lax.fori_loop(0, n_kv // KV_STEP - 1, body, 0)
    for b in range(KV_STEP):
        key_block(n_kv - KV_STEP + b, b % 2, b + 1 < KV_STEP)
    for u in range(len(units)):
        stage_scores(0, jnp.minimum(i + 1, q_tiles - 1), 0, u)

    lam = (jnp.exp(jnp.sum(lq1_ref[...] * lk1_ref[...], axis=-1, keepdims=True))
           - jnp.exp(jnp.sum(lq2_ref[...] * lk2_ref[...], axis=-1, keepdims=True)) + lam_init)
    ot = acc_ref[0] * (1.0 / l_ref[0]) - acc_ref[1] * (lam / l_ref[1])
    ms = jnp.mean(ot * ot, axis=0, keepdims=True)
    y = ((ot * lax.rsqrt(ms + EPS)) * sg_ref[...]) * (1.0 - lam_init)
    o_ref[...] = y.T.astype(BF16)


def _diff_attn(q, k, vt, lq1, lk1, lq2, lk2, sg, bsz, seq, lam_init):
    n = bsz * seq
    q_tiles = seq // Q_TILE
    n_kv = seq // KV_TILE
    assert n_kv % KV_STEP == 0 and KV_STEP % 2 == 0
    n_units = 2 * Q_TILE // Q_GROUP
    vec = pl.BlockSpec((1, B_QK_DIM), lambda b, h, i: (0, 0))
    return pl.pallas_call(
        functools.partial(_diff_attn_kernel, lam_init, n_kv, q_tiles),
        grid=(bsz, B_HEADS, q_tiles),
        in_specs=[
            pl.BlockSpec((seq, LANES), lambda b, h, i: (b, h)),
            pl.BlockSpec((seq, LANES), lambda b, h, i: (b, h)),
            pl.BlockSpec((1, 1, n_kv, B_V_DIM, KV_TILE), lambda b, h, i: (b, h, 0, 0, 0)),
            vec, vec, vec, vec,
            pl.BlockSpec((B_V_DIM, Q_TILE), lambda b, h, i: (0, 0)),
        ],
        out_specs=pl.BlockSpec((Q_TILE, LANES), lambda b, h, i: (b * q_tiles + i, h)),
        out_shape=jax.ShapeDtypeStruct((n, B_WIDTH), BF16),
        scratch_shapes=[
            pltpu.VMEM((2, 1, Q_TILE), F32),
            pltpu.VMEM((2, 1, Q_TILE), F32),
            pltpu.VMEM((2, B_V_DIM, Q_TILE), F32),
            pltpu.VMEM((2, n_units, KV_TILE, Q_GROUP), F32),
            pltpu.VMEM((2, n_units, 1, Q_GROUP), F32),
        ],
        compiler_params=pltpu.CompilerParams(
            dimension_semantics=("arbitrary", "arbitrary", "arbitrary"),
            vmem_limit_bytes=VMEM_LIMIT),
        name="diff_attn",
    )(q, k, vt, lq1, lk1, lq2, lk2, sg)


def _proj_ffn_kernel(final, h_ref, ma_ref, mb_ref, wo_ref, g_ref, w1_ref, w2_ref, fg_ref, o_ref):
    half = D_MODEL // 2
    h1 = h_ref[...] + _dot(ma_ref[...], wo_ref[0:half, :]) + _dot(mb_ref[...], wo_ref[half:, :])
    xn = _rms(h1, g_ref[...]).astype(BF16)
    ff = None
    for c in range(D_FF // FF_CHUNK):
        cols = slice(c * FF_CHUNK, (c + 1) * FF_CHUNK)
        t = jnp.square(jnp.maximum(_dot(xn, w1_ref[:, cols]), 0.0)).astype(BF16)
        d = _dot(t, w2_ref[cols, :])
        ff = d if ff is None else ff + d
    out = h1 + ff
    if final:
        out = _rms(out, fg_ref[...])
    o_ref[...] = out


def _proj_ffn(h, mix_a, mix_b, b_col, wo, g, w1, w2, fg, final):
    n = h.shape[0]
    half = D_MODEL // 2
    row = lambda i: (i, 0)
    return pl.pallas_call(
        functools.partial(_proj_ffn_kernel, final),
        grid=(n // ROW_TILE,),
        in_specs=[
            pl.BlockSpec((ROW_TILE, D_MODEL), row),
            pl.BlockSpec((ROW_TILE, half), row),
            pl.BlockSpec((ROW_TILE, half), lambda i: (i, b_col)),
            _resident((D_MODEL, D_MODEL)),
            _resident((1, D_MODEL)),
            _resident((D_MODEL, D_FF)),
            _resident((D_FF, D_MODEL)),
            _resident((1, D_MODEL)),
        ],
        out_specs=pl.BlockSpec((ROW_TILE, D_MODEL), row),
        out_shape=jax.ShapeDtypeStruct((n, D_MODEL), F32),
        compiler_params=pltpu.CompilerParams(
            dimension_semantics=("arbitrary",), vmem_limit_bytes=VMEM_LIMIT),
        name="proj_ffn_final" if final else "proj_ffn",
    )(h, mix_a, mix_b, wo, g, w1, w2, fg)


def _conv_in_kernel(tiles_per_seq, x_ref, xp_ref, xn_ref, g_ref, w_ref, cw_ref, o_ref):
    i = pl.program_id(0)
    xe = jnp.concatenate([x_ref[...], xp_ref[...], xn_ref[...]], axis=0)
    bgs, ts = [], []
    for r0 in range(0, ROW_TILE, SUB_ROWS):
        r1 = r0 + SUB_ROWS if r0 + SUB_ROWS < ROW_TILE else ROW_TILE + 2 * HALO
        xn = _rms(xe[r0:r1], g_ref[...]).astype(BF16)
        bgs.append(_dot(xn[:SUB_ROWS], w_ref[:, 0:D_MODEL]))
        ts.append(_dot(xn, w_ref[:, D_MODEL:2 * D_MODEL]) * _dot(xn, w_ref[:, 2 * D_MODEL:]))
    bg = jnp.concatenate(bgs, axis=0)
    t = jnp.concatenate(ts, axis=0)
    tm = t[:ROW_TILE]
    before = jnp.where(i % tiles_per_seq == 0, 0.0, t[ROW_TILE + HALO - 1:ROW_TILE + HALO])
    after = jnp.where(i % tiles_per_seq == tiles_per_seq - 1, 0.0,
                      t[ROW_TILE + HALO:ROW_TILE + HALO + 1])
    r = lax.broadcasted_iota(jnp.int32, tm.shape, 0)
    t_prev = jnp.where(r == 0, before, pltpu.roll(tm, 1, 0))
    t_next = jnp.where(r == ROW_TILE - 1, after, pltpu.roll(tm, ROW_TILE - 1, 0))
    y = t_prev * cw_ref[0:1, :] + tm * cw_ref[1:2, :] + t_next * cw_ref[2:3, :]
    o_ref[...] = (bg * y).astype(BF16)


def _conv_in(h, seq, g, w, cw):
    n = h.shape[0]
    tiles_per_seq = seq // ROW_TILE
    halos_per_tile = ROW_TILE // HALO
    last_halo = n // HALO - 1
    return pl.pallas_call(
        functools.partial(_conv_in_kernel, tiles_per_seq),
        grid=(n // ROW_TILE,),
        in_specs=[
            pl.BlockSpec((ROW_TILE, D_MODEL), lambda i: (i, 0)),
            pl.BlockSpec((HALO, D_MODEL), lambda i: (jnp.maximum(i * halos_per_tile - 1, 0), 0)),
            pl.BlockSpec((HALO, D_MODEL),
                         lambda i: (jnp.minimum((i + 1) * halos_per_tile, last_halo), 0)),
            _resident((1, D_MODEL)),
            _resident((D_MODEL, 3 * D_MODEL)),
            _resident((CONV_W, D_MODEL)),
        ],
        out_specs=pl.BlockSpec((ROW_TILE, D_MODEL), lambda i: (i, 0)),
        out_shape=jax.ShapeDtypeStruct((n, D_MODEL), BF16),
        compiler_params=pltpu.CompilerParams(
            dimension_semantics=("arbitrary",), vmem_limit_bytes=VMEM_LIMIT),
        name="conv_in",
    )(h, h, h, g, w, cw)


def _rope_tables(seq):
    inv = ROPE_THETA ** (-jnp.arange(0, ROT_DIM, 2, dtype=F32) / ROT_DIM)
    ang = jnp.arange(seq, dtype=F32)[:, None] * inv[None, :]
    cos, sin = jnp.cos(ang), jnp.sin(ang)
    half = ROT_DIM // 2
    lane = jnp.arange(LANES) % B_QK_DIM
    rotated = lane < ROT_DIM
    pick = ((lane % half)[None, :] == jnp.arange(half)[:, None]) & rotated[None, :]
    sign = jnp.where(lane < half, -1.0, 1.0)
    cos_t = jnp.dot(cos, pick.astype(F32), precision=lax.Precision.HIGHEST) + (~rotated).astype(F32)
    sin_t = jnp.dot(sin, pick.astype(F32) * sign, precision=lax.Precision.HIGHEST)
    return cos_t, sin_t


def _trunk(x, p, cos_t, sin_t):
    bsz, seq, _ = x.shape
    h = x.reshape(bsz * seq, D_MODEL)

    out_a, q, k, vt = _mix0_in(h, bsz, seq, p["mix_g"][0], p["e_w_in"], p["a_vnorm_g"], p["a_w_s"],
                               p["a_bias"], cos_t, sin_t)
    lam_init = 0.8 - 0.6 * math.exp(-0.3 * 0)
    out_b = _diff_attn(q, k, vt, p["b_lq1"], p["b_lk1"], p["b_lq2"], p["b_lk2"], p["b_subln_g"],
                       bsz, seq, lam_init)
    h = _proj_ffn(h, out_a, out_b, 0, p["e_w_out"], p["ffn_g"][0], p["ffn_w1"][0], p["ffn_w2"][0],
                  p["final_g"], final=False)
    mix = _conv_in(h, seq, p["mix_g"][1], p["c_w_in"], p["c_conv_w"])
    h = _proj_ffn(h, mix, mix, 1, p["c_w_out"], p["ffn_g"][1], p["ffn_w1"][1], p["ffn_w2"][1],
                  p["final_g"], final=True)
    return h.reshape(bsz, seq, D_MODEL)


def kernel(x_prompt, x_sample, norm_mix_g, norm_ffn_g, ffn_w1, ffn_w2, e_w_in, e_w_out, a_vnorm_g, a_w_s, a_b_s, b_lq1, b_lk1, b_lq2, b_lk2, b_subln_g, c_w_in, c_conv_w, c_w_out, final_g):
    depth = norm_mix_g.shape[0]
    p = {
        "mix_g": [norm_mix_g[i].reshape(1, D_MODEL) for i in range(depth)],
        "ffn_g": [norm_ffn_g[i].reshape(1, D_MODEL) for i in range(depth)],
        "ffn_w1": [ffn_w1[i].astype(BF16) for i in range(depth)],
        "ffn_w2": [ffn_w2[i].astype(BF16) for i in range(depth)],
        "e_w_in": e_w_in[0].astype(BF16),
        "e_w_out": e_w_out[0].astype(BF16),
        "a_vnorm_g": a_vnorm_g[0].reshape(1, A_WIDTH),
        "a_w_s": a_w_s[0].astype(BF16),
        "a_bias": jnp.repeat(a_b_s[0].T, A_GROUP_DIM, axis=1),
        "b_lq1": b_lq1[0].reshape(1, B_QK_DIM),
        "b_lk1": b_lk1[0].reshape(1, B_QK_DIM),
        "b_lq2": b_lq2[0].reshape(1, B_QK_DIM),
        "b_lk2": b_lk2[0].reshape(1, B_QK_DIM),
        "b_subln_g": jnp.broadcast_to(b_subln_g[0][:, None], (B_V_DIM, Q_TILE)),
        "c_w_in": c_w_in[0].astype(BF16),
        "c_conv_w": c_conv_w[0],
        "c_w_out": c_w_out[0].astype(BF16),
        "final_g": final_g.reshape(1, D_MODEL),
    }
    cos_t, sin_t = _rope_tables(max(x_prompt.shape[1], x_sample.shape[1]))
    return (_trunk(x_prompt, p, cos_t, sin_t), _trunk(x_sample, p, cos_t, sin_t))
```

```python
import functools
import math

import jax
import jax.numpy as jnp
from jax import lax
from jax.experimental import pallas as pl
from jax.experimental.pallas import tpu as pltpu

D_MODEL = 1024
D_FF = 4 * D_MODEL
EPS = 1e-5
A_WIDTH = 512
A_GROUPS = 4
A_GROUP_DIM = 128
CHUNK = 128
B_HEADS = 4
B_QK_DIM = 64
B_V_DIM = 128
B_WIDTH = 512
ROT_DIM = 16
ROPE_THETA = 500000.0
E_IN = 2 * A_WIDTH + 2 * B_WIDTH + B_WIDTH
CONV_W = 3

LANES = 128
ROW_TILE = 512
SUB_ROWS = 256
CONV_CHAINS = 2
KV_TILE = 512
Q_TILE = 1024
Q_GROUP = 256
KV_STEP = 4
SUM_ROWS = 16
HALO = 16
FF_CHUNK = 1024
VMEM_LIMIT = 56 * 1024 * 1024

F32 = jnp.float32
BF16 = jnp.bfloat16


def _resident(shape):
    return pl.BlockSpec(shape, lambda *_: (0,) * len(shape), pipeline_mode=pl.Buffered(1))


def _rms(x, g):
    ms = jnp.mean(x * x, axis=-1, keepdims=True)
    return (x * lax.rsqrt(ms + EPS)) * g


def _dot(a, b):
    return jnp.dot(a, b, preferred_element_type=F32)


def _reduce_rows(x, op, width=32):
    acc = x[0:width]
    for r in range(width, x.shape[0], width):
        acc = op(acc, x[r:r + width])
    while acc.shape[0] > 8:
        half = acc.shape[0] // 2
        acc = op(acc[:half], acc[half:])
    reduce = jnp.max if op is jnp.maximum else jnp.sum
    return reduce(acc, axis=0, keepdims=True)


def _mix0_in_kernel(x_ref, g_ref, w_ref, vg_ref, ws_ref, bias_ref, cos_ref, sin_ref,
                    oa_ref, q_ref, k_ref, vt_ref):
    lane = lax.broadcasted_iota(jnp.int32, (SUB_ROWS, LANES), 1)
    upper_half = (lane % B_QK_DIM) >= ROT_DIM // 2
    scale = math.log2(math.e) / math.sqrt(B_QK_DIM)
    q0, k0, v0 = 2 * A_WIDTH, 2 * A_WIDTH + B_WIDTH, 2 * A_WIDTH + 2 * B_WIDTH

    pieces = []
    for r0 in range(0, ROW_TILE, SUB_ROWS):
        rows = slice(r0, r0 + SUB_ROWS)
        xn = _rms(x_ref[rows, :], g_ref[...]).astype(BF16)
        pieces.append((rows, [_dot(xn, w_ref[:, lo:hi]) for lo, hi in
                              ((A_WIDTH, q0), (0, A_WIDTH), (q0, k0), (k0, v0), (v0, E_IN))]))

    for rows, (av, au, q, k, v) in pieces:
        u = jax.nn.gelu(au)
        gv = jax.nn.gelu(av)
        for g in range(A_GROUPS):
            cols = slice(g * A_GROUP_DIM, (g + 1) * A_GROUP_DIM)
            vn = _rms(gv[:, cols], vg_ref[:, cols]).astype(BF16)
            for c in range(SUB_ROWS // CHUNK):
                sub = slice(c * CHUNK, (c + 1) * CHUNK)
                mixed = _dot(ws_ref[g], vn[sub, :]) + bias_ref[:, cols]
                oa_ref[rows.start + c * CHUNK:rows.start + (c + 1) * CHUNK, cols] = (
                    u[sub, cols] * mixed).astype(BF16)

        cos = cos_ref[rows, :]
        sin = sin_ref[rows, :]

        def rope(t):
            partner = jnp.where(upper_half, pltpu.roll(t, ROT_DIM // 2, 1),
                                pltpu.roll(t, LANES - ROT_DIM // 2, 1))
            return t * cos + partner * sin

        for h in range(B_HEADS):
            cols = slice(h * LANES, (h + 1) * LANES)
            q_ref[rows, cols] = (rope(q[:, cols]) * scale).astype(BF16)
            k_ref[rows, cols] = rope(k[:, cols]).astype(BF16)

        vt = v.T
        for h in range(B_HEADS):
            vt_ref[0, h, 0, :, rows] = vt[h * B_V_DIM:(h + 1) * B_V_DIM, :].astype(BF16)


def _mix0_in(x2d, bsz, seq, g, w, vg, ws, bias, cos_t, sin_t):
    n = bsz * seq
    tiles_per_seq = seq // ROW_TILE
    row = lambda i: (i, 0)
    return pl.pallas_call(
        _mix0_in_kernel,
        grid=(n // ROW_TILE,),
        in_specs=[
            pl.BlockSpec((ROW_TILE, D_MODEL), row),
            _resident((1, D_MODEL)),
            _resident((D_MODEL, E_IN)),
            _resident((1, A_WIDTH)),
            _resident((A_GROUPS, CHUNK, CHUNK)),
            _resident((CHUNK, A_WIDTH)),
            pl.BlockSpec((ROW_TILE, LANES), lambda i: (i % tiles_per_seq, 0)),
            pl.BlockSpec((ROW_TILE, LANES), lambda i: (i % tiles_per_seq, 0)),
        ],
        out_specs=[
            pl.BlockSpec((ROW_TILE, A_WIDTH), row),
            pl.BlockSpec((ROW_TILE, B_WIDTH), row),
            pl.BlockSpec((ROW_TILE, B_WIDTH), row),
            pl.BlockSpec((1, B_HEADS, 1, B_V_DIM, KV_TILE),
                         lambda i: (i // tiles_per_seq, 0, i % tiles_per_seq, 0, 0)),
        ],
        out_shape=[
            jax.ShapeDtypeStruct((n, A_WIDTH), BF16),
            jax.ShapeDtypeStruct((n, B_WIDTH), BF16),
            jax.ShapeDtypeStruct((n, B_WIDTH), BF16),
            jax.ShapeDtypeStruct((bsz, B_HEADS, seq // KV_TILE, B_V_DIM, KV_TILE), BF16),
        ],
        compiler_params=pltpu.CompilerParams(
            dimension_semantics=("arbitrary",), vmem_limit_bytes=VMEM_LIMIT),
        name="mix0_in",
    )(x2d, g, w, vg, ws, bias, cos_t, sin_t)


def _diff_attn_kernel(lam_init, n_kv, q_tiles, q_ref, k_ref, vt_ref, lq1_ref, lk1_ref, lq2_ref,
                      lk2_ref, sg_ref, o_ref, m_ref, acc_ref, s_ref, mb_ref):
    i = pl.program_id(2)
    units = [(s, c) for s in range(2) for c in range(Q_TILE // Q_GROUP)]
    lane = lax.broadcasted_iota(jnp.int32, (Q_GROUP, LANES), 1)
    stream_lanes = (lane < B_QK_DIM, lane >= B_QK_DIM)

    def stage_scores(slot, qt, jb, u):
        s, c = units[u]
        q_row = pl.multiple_of(qt * Q_TILE + c * Q_GROUP, Q_GROUP)
        qg = q_ref[pl.ds(q_row, Q_GROUP), :]
        qg = jnp.where(stream_lanes[s], qg, jnp.zeros_like(qg))
        kb = k_ref[pl.ds(pl.multiple_of(jb * KV_TILE, KV_TILE), KV_TILE), :]
        st = lax.dot_general(kb, qg, (((1,), (1,)), ((), ())),
                             preferred_element_type=F32)
        s_ref[slot, u] = st
        mb_ref[slot, u] = _reduce_rows(st, jnp.maximum)

    @pl.when(i == 0)
    def _():
        for u in range(len(units)):
            stage_scores(0, 0, 0, u)

    m_ref[...] = jnp.full(m_ref.shape, -jnp.inf, F32)
    acc_ref[...] = jnp.zeros(acc_ref.shape, F32)

    def key_block(jb, slot, prefetch_next):
        vtb = jnp.concatenate([vt_ref[0, 0, jb], jnp.ones((SUM_ROWS, KV_TILE), BF16)], axis=0)
        for u, (s, c) in enumerate(units):
            cols = slice(c * Q_GROUP, (c + 1) * Q_GROUP)
            if prefetch_next:
                stage_scores(1 - slot, i, jb + 1, u)
            m_old = m_ref[s, :, cols]
            m_new = jnp.maximum(m_old, mb_ref[slot, u])
            alpha = jnp.exp2(m_old - m_new)
            p = jnp.exp2(s_ref[slot, u] - m_new)
            acc_ref[s, :, cols] = alpha * acc_ref[s, :, cols] + _dot(vtb, p.astype(BF16))
            m_ref[s, :, cols] = m_new

    def body(t, carry):
        for b in range(KV_STEP):
            key_block(t * KV_STEP + b, b % 2, True)
        return carry

    lax.fori_loop(0, n_kv // KV_STEP - 1, body, 0)
    for b in range(KV_STEP):
        key_block(n_kv - KV_STEP + b, b % 2, b + 1 < KV_STEP)
    for u in range(len(units)):
        stage_scores(0, jnp.minimum(i + 1, q_tiles - 1), 0, u)

    lam = (jnp.exp(jnp.sum(lq1_ref[...] * lk1_ref[...], axis=-1, keepdims=True))
           - jnp.exp(jnp.sum(lq2_ref[...] * lk2_ref[...], axis=-1, keepdims=True)) + lam_init)
    l1 = acc_ref[0, B_V_DIM:B_V_DIM + 1, :]
    l2 = acc_ref[1, B_V_DIM:B_V_DIM + 1, :]
    ot = acc_ref[0, 0:B_V_DIM, :] * (1.0 / l1) - acc_ref[1, 0:B_V_DIM, :] * (lam / l2)
    ms = jnp.mean(ot * ot, axis=0, keepdims=True)
    y = ((ot * lax.rsqrt(ms + EPS)) * sg_ref[...]) * (1.0 - lam_init)
    o_ref[...] = y.T.astype(BF16)


def _diff_attn(q, k, vt, lq1, lk1, lq2, lk2, sg, bsz, seq, lam_init):
    n = bsz * seq
    q_tiles = seq // Q_TILE
    n_kv = seq // KV_TILE
    assert n_kv % KV_STEP == 0 and KV_STEP % 2 == 0
    n_units = 2 * Q_TILE // Q_GROUP
    vec = pl.BlockSpec((1, B_QK_DIM), lambda b, h, i: (0, 0))
    return pl.pallas_call(
        functools.partial(_diff_attn_kernel, lam_init, n_kv, q_tiles),
        grid=(bsz, B_HEADS, q_tiles),
        in_specs=[
            pl.BlockSpec((seq, LANES), lambda b, h, i: (b, h)),
            pl.BlockSpec((seq, LANES), lambda b, h, i: (b, h)),
            pl.BlockSpec((1, 1, n_kv, B_V_DIM, KV_TILE), lambda b, h, i: (b, h, 0, 0, 0)),
            vec, vec, vec, vec,
            pl.BlockSpec((B_V_DIM, Q_TILE), lambda b, h, i: (0, 0)),
        ],
        out_specs=pl.BlockSpec((Q_TILE, LANES), lambda b, h, i: (b * q_tiles + i, h)),
        out_shape=jax.ShapeDtypeStruct((n, B_WIDTH), BF16),
        scratch_shapes=[
            pltpu.VMEM((2, 1, Q_TILE), F32),
            pltpu.VMEM((2, B_V_DIM + SUM_ROWS, Q_TILE), F32),
            pltpu.VMEM((2, n_units, KV_TILE, Q_GROUP), F32),
            pltpu.VMEM((2, n_units, 1, Q_GROUP), F32),
        ],
        compiler_params=pltpu.CompilerParams(
            dimension_semantics=("arbitrary", "arbitrary", "arbitrary"),
            vmem_limit_bytes=VMEM_LIMIT),
        name="diff_attn",
    )(q, k, vt, lq1, lk1, lq2, lk2, sg)


def _ffn_block(h1, g_ref, w1_ref, w2_ref):
    xn = _rms(h1, g_ref[...]).astype(BF16)
    ff = None
    for c in range(D_FF // FF_CHUNK):
        cols = slice(c * FF_CHUNK, (c + 1) * FF_CHUNK)
        t = jnp.square(jnp.maximum(_dot(xn, w1_ref[:, cols]), 0.0)).astype(BF16)
        d = _dot(t, w2_ref[cols, :])
        ff = d if ff is None else ff + d
    return h1 + ff


def _proj_ffn_conv_kernel(h_ref, ma_ref, mb_ref, wo_ref, g_ref, w1_ref, w2_ref, cg_ref, cw_ref,
                          ho_ref, bg_ref, t_ref):
    half = D_MODEL // 2
    h1 = h_ref[...] + _dot(ma_ref[...], wo_ref[0:half, :]) + _dot(mb_ref[...], wo_ref[half:, :])
    h2 = _ffn_block(h1, g_ref, w1_ref, w2_ref)
    ho_ref[...] = h2
    xn = _rms(h2, cg_ref[...]).astype(BF16)
    bg_ref[...] = _dot(xn, cw_ref[:, 0:D_MODEL]).astype(BF16)
    t_ref[...] = (_dot(xn, cw_ref[:, D_MODEL:2 * D_MODEL])
                  * _dot(xn, cw_ref[:, 2 * D_MODEL:])).astype(BF16)


def _proj_ffn_conv(h, mix_a, mix_b, wo, g, w1, w2, cg, cw):
    n = h.shape[0]
    half = D_MODEL // 2
    row = lambda i: (i, 0)
    tile_f32 = pl.BlockSpec((ROW_TILE, D_MODEL), row)
    return pl.pallas_call(
        _proj_ffn_conv_kernel,
        grid=(n // ROW_TILE,),
        in_specs=[
            tile_f32,
            pl.BlockSpec((ROW_TILE, half), row),
            pl.BlockSpec((ROW_TILE, half), row),
            _resident((D_MODEL, D_MODEL)),
            _resident((1, D_MODEL)),
            _resident((D_MODEL, D_FF)),
            _resident((D_FF, D_MODEL)),
            _resident((1, D_MODEL)),
            _resident((D_MODEL, 3 * D_MODEL)),
        ],
        out_specs=[tile_f32, pl.BlockSpec((ROW_TILE, D_MODEL), row),
                   pl.BlockSpec((ROW_TILE, D_MODEL), row)],
        out_shape=[jax.ShapeDtypeStruct((n, D_MODEL), F32), jax.ShapeDtypeStruct((n, D_MODEL), BF16),
                   jax.ShapeDtypeStruct((n, D_MODEL), BF16)],
        compiler_params=pltpu.CompilerParams(
            dimension_semantics=("arbitrary",), vmem_limit_bytes=VMEM_LIMIT),
        name="proj_ffn_conv",
    )(h, mix_a, mix_b, wo, g, w1, w2, cg, cw)


def _conv_ffn_final_kernel(tiles_per_seq, h_ref, bg_ref, t_ref, tp_ref, tn_ref, cw_ref, wo_ref,
                           g_ref, w1_ref, w2_ref, fg_ref, o_ref):
    i = pl.program_id(0)
    t = t_ref[...].astype(F32)
    before = jnp.where(i % tiles_per_seq == 0, 0.0, tp_ref[HALO - 1:HALO, :].astype(F32))
    after = jnp.where(i % tiles_per_seq == tiles_per_seq - 1, 0.0, tn_ref[0:1, :].astype(F32))
    rows_total = t.shape[0]
    r = lax.broadcasted_iota(jnp.int32, t.shape, 0)
    t_prev = jnp.where(r == 0, before, pltpu.roll(t, 1, 0))
    t_next = jnp.where(r == rows_total - 1, after, pltpu.roll(t, rows_total - 1, 0))
    y = t_prev * cw_ref[0:1, :] + t * cw_ref[1:2, :] + t_next * cw_ref[2:3, :]
    for r0 in range(0, rows_total, ROW_TILE):
        rows = slice(r0, r0 + ROW_TILE)
        mix = (bg_ref[rows, :].astype(F32) * y[rows]).astype(BF16)
        h1 = h_ref[rows, :] + _dot(mix, wo_ref[...])
        o_ref[rows, :] = _rms(_ffn_block(h1, g_ref, w1_ref, w2_ref), fg_ref[...])


def _conv_ffn_final(h, bg, t, seq, cw, wo, g, w1, w2, fg):
    n = h.shape[0]
    tile_rows = CONV_CHAINS * ROW_TILE
    tiles_per_seq = seq // tile_rows
    halos_per_tile = tile_rows // HALO
    last_halo = n // HALO - 1
    row = lambda i: (i, 0)
    tile = pl.BlockSpec((tile_rows, D_MODEL), row)
    return pl.pallas_call(
        functools.partial(_conv_ffn_final_kernel, tiles_per_seq),
        grid=(n // tile_rows,),
        in_specs=[
            tile, tile, tile,
            pl.BlockSpec((HALO, D_MODEL), lambda i: (jnp.maximum(i * halos_per_tile - 1, 0), 0)),
            pl.BlockSpec((HALO, D_MODEL),
                         lambda i: (jnp.minimum((i + 1) * halos_per_tile, last_halo), 0)),
            _resident((CONV_W, D_MODEL)),
            _resident((D_MODEL, D_MODEL)),
            _resident((1, D_MODEL)),
            _resident((D_MODEL, D_FF)),
            _resident((D_FF, D_MODEL)),
            _resident((1, D_MODEL)),
        ],
        out_specs=tile,
        out_shape=jax.ShapeDtypeStruct((n, D_MODEL), F32),
        compiler_params=pltpu.CompilerParams(
            dimension_semantics=("arbitrary",), vmem_limit_bytes=VMEM_LIMIT),
        name="conv_ffn_final",
    )(h, bg, t, t, t, cw, wo, g, w1, w2, fg)


def _rope_tables(seq):
    inv = ROPE_THETA ** (-jnp.arange(0, ROT_DIM, 2, dtype=F32) / ROT_DIM)
    ang = jnp.arange(seq, dtype=F32)[:, None] * inv[None, :]
    cos, sin = jnp.cos(ang), jnp.sin(ang)
    half = ROT_DIM // 2
    lane = jnp.arange(LANES) % B_QK_DIM
    rotated = lane < ROT_DIM
    pick = ((lane % half)[None, :] == jnp.arange(half)[:, None]) & rotated[None, :]
    sign = jnp.where(lane < half, -1.0, 1.0)
    cos_t = jnp.dot(cos, pick.astype(F32), precision=lax.Precision.HIGHEST) + (~rotated).astype(F32)
    sin_t = jnp.dot(sin, pick.astype(F32) * sign, precision=lax.Precision.HIGHEST)
    return cos_t, sin_t


def _trunk(x, p, cos_t, sin_t):
    bsz, seq, _ = x.shape
    h = x.reshape(bsz * seq, D_MODEL)

    out_a, q, k, vt = _mix0_in(h, bsz, seq, p["mix_g"][0], p["e_w_in"], p["a_vnorm_g"], p["a_w_s"],
                               p["a_bias"], cos_t, sin_t)
    lam_init = 0.8 - 0.6 * math.exp(-0.3 * 0)
    out_b = _diff_attn(q, k, vt, p["b_lq1"], p["b_lk1"], p["b_lq2"], p["b_lk2"], p["b_subln_g"],
                       bsz, seq, lam_init)
    h, bg, t = _proj_ffn_conv(h, out_a, out_b, p["e_w_out"], p["ffn_g"][0], p["ffn_w1"][0],
                              p["ffn_w2"][0], p["mix_g"][1], p["c_w_in"])
    h = _conv_ffn_final(h, bg, t, seq, p["c_conv_w"], p["c_w_out"], p["ffn_g"][1], p["ffn_w1"][1],
                        p["ffn_w2"][1], p["final_g"])
    return h.reshape(bsz, seq, D_MODEL)


def kernel(x_prompt, x_sample, norm_mix_g, norm_ffn_g, ffn_w1, ffn_w2, e_w_in, e_w_out, a_vnorm_g, a_w_s, a_b_s, b_lq1, b_lk1, b_lq2, b_lk2, b_subln_g, c_w_in, c_conv_w, c_w_out, final_g):
    depth = norm_mix_g.shape[0]
    p = {
        "mix_g": [norm_mix_g[i].reshape(1, D_MODEL) for i in range(depth)],
        "ffn_g": [norm_ffn_g[i].reshape(1, D_MODEL) for i in range(depth)],
        "ffn_w1": [ffn_w1[i].astype(BF16) for i in range(depth)],
        "ffn_w2": [ffn_w2[i].astype(BF16) for i in range(depth)],
        "e_w_in": e_w_in[0].astype(BF16),
        "e_w_out": e_w_out[0].astype(BF16),
        "a_vnorm_g": a_vnorm_g[0].reshape(1, A_WIDTH),
        "a_w_s": a_w_s[0].astype(BF16),
        "a_bias": jnp.repeat(a_b_s[0].T, A_GROUP_DIM, axis=1),
        "b_lq1": b_lq1[0].reshape(1, B_QK_DIM),
        "b_lk1": b_lk1[0].reshape(1, B_QK_DIM),
        "b_lq2": b_lq2[0].reshape(1, B_QK_DIM),
        "b_lk2": b_lk2[0].reshape(1, B_QK_DIM),
        "b_subln_g": jnp.broadcast_to(b_subln_g[0][:, None], (B_V_DIM, Q_TILE)),
        "c_w_in": c_w_in[0].astype(BF16),
        "c_conv_w": c_conv_w[0],
        "c_w_out": c_w_out[0].astype(BF16),
        "final_g": final_g.reshape(1, D_MODEL),
    }
    cos_t, sin_t = _rope_tables(max(x_prompt.shape[1], x_sample.shape[1]))
    return (_trunk(x_prompt, p, cos_t, sin_t), _trunk(x_sample, p, cos_t, sin_t))
```

```python
import functools
import math

import jax
import jax.numpy as jnp
from jax import lax
from jax.experimental import pallas as pl
from jax.experimental.pallas import tpu as pltpu

D_MODEL = 1024
D_FF = 4 * D_MODEL
EPS = 1e-5
A_WIDTH = 512
A_GROUPS = 4
A_GROUP_DIM = 128
CHUNK = 128
B_HEADS = 4
B_QK_DIM = 64
B_V_DIM = 128
B_WIDTH = 512
ROT_DIM = 16
ROPE_THETA = 500000.0
E_IN = 2 * A_WIDTH + 2 * B_WIDTH + B_WIDTH
CONV_W = 3

LANES = 128
ROW_TILE = 512
SUB_ROWS = 256
CONV_CHAINS = 2
KV_TILE = 1024
Q_TILE = 1024
Q_GROUP = 256
KV_STEP = 2
SUM_ROWS = 16
HALO = 16
FF_CHUNK = 1024
VMEM_LIMIT = 56 * 1024 * 1024

F32 = jnp.float32
BF16 = jnp.bfloat16


def _resident(shape):
    return pl.BlockSpec(shape, lambda *_: (0,) * len(shape), pipeline_mode=pl.Buffered(1))


def _rms(x, g):
    ms = jnp.mean(x * x, axis=-1, keepdims=True)
    return (x * lax.rsqrt(ms + EPS)) * g


def _dot(a, b):
    return jnp.dot(a, b, preferred_element_type=F32)


def _reduce_rows(x, op, width=32):
    acc = x[0:width]
    for r in range(width, x.shape[0], width):
        acc = op(acc, x[r:r + width])
    while acc.shape[0] > 8:
        half = acc.shape[0] // 2
        acc = op(acc[:half], acc[half:])
    reduce = jnp.max if op is jnp.maximum else jnp.sum
    return reduce(acc, axis=0, keepdims=True)


def _mix0_in_kernel(x_ref, g_ref, w_ref, vg_ref, ws_ref, bias_ref, cos_ref, sin_ref,
                    oa_ref, q_ref, k_ref, vt_ref):
    lane = lax.broadcasted_iota(jnp.int32, (SUB_ROWS, LANES), 1)
    upper_half = (lane % B_QK_DIM) >= ROT_DIM // 2
    scale = math.log2(math.e) / math.sqrt(B_QK_DIM)
    q0, k0, v0 = 2 * A_WIDTH, 2 * A_WIDTH + B_WIDTH, 2 * A_WIDTH + 2 * B_WIDTH

    pieces = []
    for r0 in range(0, ROW_TILE, SUB_ROWS):
        rows = slice(r0, r0 + SUB_ROWS)
        xn = _rms(x_ref[rows, :], g_ref[...]).astype(BF16)
        pieces.append((rows, [_dot(xn, w_ref[:, lo:hi]) for lo, hi in
                              ((A_WIDTH, q0), (0, A_WIDTH), (q0, k0), (k0, v0), (v0, E_IN))]))

    for rows, (av, au, q, k, v) in pieces:
        u = jax.nn.gelu(au)
        gv = jax.nn.gelu(av)
        for g in range(A_GROUPS):
            cols = slice(g * A_GROUP_DIM, (g + 1) * A_GROUP_DIM)
            vn = _rms(gv[:, cols], vg_ref[:, cols]).astype(BF16)
            for c in range(SUB_ROWS // CHUNK):
                sub = slice(c * CHUNK, (c + 1) * CHUNK)
                mixed = _dot(ws_ref[g], vn[sub, :]) + bias_ref[:, cols]
                oa_ref[rows.start + c * CHUNK:rows.start + (c + 1) * CHUNK, cols] = (
                    u[sub, cols] * mixed).astype(BF16)

        cos = cos_ref[rows, :]
        sin = sin_ref[rows, :]

        def rope(t):
            partner = jnp.where(upper_half, pltpu.roll(t, ROT_DIM // 2, 1),
                                pltpu.roll(t, LANES - ROT_DIM // 2, 1))
            return t * cos + partner * sin

        for h in range(B_HEADS):
            cols = slice(h * LANES, (h + 1) * LANES)
            q_ref[rows, cols] = (rope(q[:, cols]) * scale).astype(BF16)
            k_ref[rows, cols] = rope(k[:, cols]).astype(BF16)

        vt = v.T
        for h in range(B_HEADS):
            vt_ref[0, h, 0, :, rows] = vt[h * B_V_DIM:(h + 1) * B_V_DIM, :].astype(BF16)


def _mix0_in(x2d, bsz, seq, g, w, vg, ws, bias, cos_t, sin_t):
    n = bsz * seq
    tiles_per_seq = seq // ROW_TILE
    tiles_per_kv = KV_TILE // ROW_TILE
    row = lambda i: (i, 0)
    return pl.pallas_call(
        _mix0_in_kernel,
        grid=(n // ROW_TILE,),
        in_specs=[
            pl.BlockSpec((ROW_TILE, D_MODEL), row),
            _resident((1, D_MODEL)),
            _resident((D_MODEL, E_IN)),
            _resident((1, A_WIDTH)),
            _resident((A_GROUPS, CHUNK, CHUNK)),
            _resident((CHUNK, A_WIDTH)),
            pl.BlockSpec((ROW_TILE, LANES), lambda i: (i % tiles_per_seq, 0)),
            pl.BlockSpec((ROW_TILE, LANES), lambda i: (i % tiles_per_seq, 0)),
        ],
        out_specs=[
            pl.BlockSpec((ROW_TILE, A_WIDTH), row),
            pl.BlockSpec((ROW_TILE, B_WIDTH), row),
            pl.BlockSpec((ROW_TILE, B_WIDTH), row),
            pl.BlockSpec((1, B_HEADS, 1, B_V_DIM, ROW_TILE),
                         lambda i: (i // tiles_per_seq, 0, (i % tiles_per_seq) // tiles_per_kv, 0,
                                    i % tiles_per_kv)),
        ],
        out_shape=[
            jax.ShapeDtypeStruct((n, A_WIDTH), BF16),
            jax.ShapeDtypeStruct((n, B_WIDTH), BF16),
            jax.ShapeDtypeStruct((n, B_WIDTH), BF16),
            jax.ShapeDtypeStruct((bsz, B_HEADS, seq // KV_TILE, B_V_DIM, KV_TILE), BF16),
        ],
        compiler_params=pltpu.CompilerParams(
            dimension_semantics=("arbitrary",), vmem_limit_bytes=VMEM_LIMIT),
        name="mix0_in",
    )(x2d, g, w, vg, ws, bias, cos_t, sin_t)


def _diff_attn_kernel(lam_init, n_kv, q_tiles, q_ref, k_ref, vt_ref, lq1_ref, lk1_ref, lq2_ref,
                      lk2_ref, sg_ref, o_ref, m_ref, acc_ref, s_ref, mb_ref):
    i = pl.program_id(2)
    units = [(s, c) for s in range(2) for c in range(Q_TILE // Q_GROUP)]
    lane = lax.broadcasted_iota(jnp.int32, (Q_GROUP, LANES), 1)
    stream_lanes = (lane < B_QK_DIM, lane >= B_QK_DIM)

    def stage_scores(slot, qt, jb, u):
        s, c = units[u]
        q_row = pl.multiple_of(qt * Q_TILE + c * Q_GROUP, Q_GROUP)
        qg = q_ref[pl.ds(q_row, Q_GROUP), :]
        qg = jnp.where(stream_lanes[s], qg, jnp.zeros_like(qg))
        kb = k_ref[pl.ds(pl.multiple_of(jb * KV_TILE, KV_TILE), KV_TILE), :]
        st = lax.dot_general(kb, qg, (((1,), (1,)), ((), ())),
                             preferred_element_type=F32)
        s_ref[slot, u] = st
        mb_ref[slot, u] = _reduce_rows(st, jnp.maximum)

    @pl.when(i == 0)
    def _():
        for u in range(len(units)):
            stage_scores(0, 0, 0, u)

    m_ref[...] = jnp.full(m_ref.shape, -jnp.inf, F32)
    acc_ref[...] = jnp.zeros(acc_ref.shape, F32)

    def key_block(jb, slot, prefetch_next):
        vtb = jnp.concatenate([vt_ref[0, 0, jb], jnp.ones((SUM_ROWS, KV_TILE), BF16)], axis=0)
        for u, (s, c) in enumerate(units):
            cols = slice(c * Q_GROUP, (c + 1) * Q_GROUP)
            if prefetch_next:
                stage_scores(1 - slot, i, jb + 1, u)
            m_old = m_ref[s, :, cols]
            m_new = jnp.maximum(m_old, mb_ref[slot, u])
            alpha = jnp.exp2(m_old - m_new)
            p = jnp.exp2(s_ref[slot, u] - m_new)
            acc_ref[s, :, cols] = alpha * acc_ref[s, :, cols] + _dot(vtb, p.astype(BF16))
            m_ref[s, :, cols] = m_new

    def body(t, carry):
        for b in range(KV_STEP):
            key_block(t * KV_STEP + b, b % 2, True)
        return carry

    lax.fori_loop(0, n_kv // KV_STEP - 1, body, 0)
    for b in range(KV_STEP):
        key_block(n_kv - KV_STEP + b, b % 2, b + 1 < KV_STEP)
    for u in range(len(units)):
        stage_scores(0, jnp.minimum(i + 1, q_tiles - 1), 0, u)

    lam = (jnp.exp(jnp.sum(lq1_ref[...] * lk1_ref[...], axis=-1, keepdims=True))
           - jnp.exp(jnp.sum(lq2_ref[...] * lk2_ref[...], axis=-1, keepdims=True)) + lam_init)
    l1 = acc_ref[0, B_V_DIM:B_V_DIM + 1, :]
    l2 = acc_ref[1, B_V_DIM:B_V_DIM + 1, :]
    ot = acc_ref[0, 0:B_V_DIM, :] * (1.0 / l1) - acc_ref[1, 0:B_V_DIM, :] * (lam / l2)
    ms = jnp.mean(ot * ot, axis=0, keepdims=True)
    y = ((ot * lax.rsqrt(ms + EPS)) * sg_ref[...]) * (1.0 - lam_init)
    o_ref[...] = y.T.astype(BF16)


def _diff_attn(q, k, vt, lq1, lk1, lq2, lk2, sg, bsz, seq, lam_init):
    n = bsz * seq
    q_tiles = seq // Q_TILE
    n_kv = seq // KV_TILE
    assert n_kv % KV_STEP == 0 and KV_STEP % 2 == 0
    n_units = 2 * Q_TILE // Q_GROUP
    vec = pl.BlockSpec((1, B_QK_DIM), lambda b, h, i: (0, 0))
    return pl.pallas_call(
        functools.partial(_diff_attn_kernel, lam_init, n_kv, q_tiles),
        grid=(bsz, B_HEADS, q_tiles),
        in_specs=[
            pl.BlockSpec((seq, LANES), lambda b, h, i: (b, h)),
            pl.BlockSpec((seq, LANES), lambda b, h, i: (b, h)),
            pl.BlockSpec((1, 1, n_kv, B_V_DIM, KV_TILE), lambda b, h, i: (b, h, 0, 0, 0)),
            vec, vec, vec, vec,
            pl.BlockSpec((B_V_DIM, Q_TILE), lambda b, h, i: (0, 0)),
        ],
        out_specs=pl.BlockSpec((Q_TILE, LANES), lambda b, h, i: (b * q_tiles + i, h)),
        out_shape=jax.ShapeDtypeStruct((n, B_WIDTH), BF16),
        scratch_shapes=[
            pltpu.VMEM((2, 1, Q_TILE), F32),
            pltpu.VMEM((2, B_V_DIM + SUM_ROWS, Q_TILE), F32),
            pltpu.VMEM((2, n_units, KV_TILE, Q_GROUP), F32),
            pltpu.VMEM((2, n_units, 1, Q_GROUP), F32),
        ],
        compiler_params=pltpu.CompilerParams(
            dimension_semantics=("arbitrary", "arbitrary", "arbitrary"),
            vmem_limit_bytes=VMEM_LIMIT),
        name="diff_attn",
    )(q, k, vt, lq1, lk1, lq2, lk2, sg)


def _ffn_block(h1, g_ref, w1_ref, w2_ref):
    xn = _rms(h1, g_ref[...]).astype(BF16)
    ff = None
    for c in range(D_FF // FF_CHUNK):
        cols = slice(c * FF_CHUNK, (c + 1) * FF_CHUNK)
        t = jnp.square(jnp.maximum(_dot(xn, w1_ref[:, cols]), 0.0)).astype(BF16)
        d = _dot(t, w2_ref[cols, :])
        ff = d if ff is None else ff + d
    return h1 + ff


def _proj_ffn_conv_kernel(h_ref, ma_ref, mb_ref, wo_ref, g_ref, w1_ref, w2_ref, cg_ref, cw_ref,
                          ho_ref, bg_ref, t_ref):
    half = D_MODEL // 2
    h1 = h_ref[...] + _dot(ma_ref[...], wo_ref[0:half, :]) + _dot(mb_ref[...], wo_ref[half:, :])
    h2 = _ffn_block(h1, g_ref, w1_ref, w2_ref)
    ho_ref[...] = h2
    xn = _rms(h2, cg_ref[...]).astype(BF16)
    bg_ref[...] = _dot(xn, cw_ref[:, 0:D_MODEL]).astype(BF16)
    t_ref[...] = (_dot(xn, cw_ref[:, D_MODEL:2 * D_MODEL])
                  * _dot(xn, cw_ref[:, 2 * D_MODEL:])).astype(BF16)


def _proj_ffn_conv(h, mix_a, mix_b, wo, g, w1, w2, cg, cw):
    n = h.shape[0]
    half = D_MODEL // 2
    row = lambda i: (i, 0)
    tile_f32 = pl.BlockSpec((ROW_TILE, D_MODEL), row)
    return pl.pallas_call(
        _proj_ffn_conv_kernel,
        grid=(n // ROW_TILE,),
        in_specs=[
            tile_f32,
            pl.BlockSpec((ROW_TILE, half), row),
            pl.BlockSpec((ROW_TILE, half), row),
            _resident((D_MODEL, D_MODEL)),
            _resident((1, D_MODEL)),
            _resident((D_MODEL, D_FF)),
            _resident((D_FF, D_MODEL)),
            _resident((1, D_MODEL)),
            _resident((D_MODEL, 3 * D_MODEL)),
        ],
        out_specs=[tile_f32, pl.BlockSpec((ROW_TILE, D_MODEL), row),
                   pl.BlockSpec((ROW_TILE, D_MODEL), row)],
        out_shape=[jax.ShapeDtypeStruct((n, D_MODEL), F32), jax.ShapeDtypeStruct((n, D_MODEL), BF16),
                   jax.ShapeDtypeStruct((n, D_MODEL), BF16)],
        compiler_params=pltpu.CompilerParams(
            dimension_semantics=("arbitrary",), vmem_limit_bytes=VMEM_LIMIT),
        name="proj_ffn_conv",
    )(h, mix_a, mix_b, wo, g, w1, w2, cg, cw)


def _conv_ffn_final_kernel(tiles_per_seq, h_ref, bg_ref, t_ref, tp_ref, tn_ref, cw_ref, wo_ref,
                           g_ref, w1_ref, w2_ref, fg_ref, o_ref):
    i = pl.program_id(0)
    t = t_ref[...].astype(F32)
    before = jnp.where(i % tiles_per_seq == 0, 0.0, tp_ref[HALO - 1:HALO, :].astype(F32))
    after = jnp.where(i % tiles_per_seq == tiles_per_seq - 1, 0.0, tn_ref[0:1, :].astype(F32))
    rows_total = t.shape[0]
    r = lax.broadcasted_iota(jnp.int32, t.shape, 0)
    t_prev = jnp.where(r == 0, before, pltpu.roll(t, 1, 0))
    t_next = jnp.where(r == rows_total - 1, after, pltpu.roll(t, rows_total - 1, 0))
    y = t_prev * cw_ref[0:1, :] + t * cw_ref[1:2, :] + t_next * cw_ref[2:3, :]
    for r0 in range(0, rows_total, ROW_TILE):
        rows = slice(r0, r0 + ROW_TILE)
        mix = (bg_ref[rows, :].astype(F32) * y[rows]).astype(BF16)
        h1 = h_ref[rows, :] + _dot(mix, wo_ref[...])
        o_ref[rows, :] = _rms(_ffn_block(h1, g_ref, w1_ref, w2_ref), fg_ref[...])


def _conv_ffn_final(h, bg, t, seq, cw, wo, g, w1, w2, fg):
    n = h.shape[0]
    tile_rows = CONV_CHAINS * ROW_TILE
    tiles_per_seq = seq // tile_rows
    halos_per_tile = tile_rows // HALO
    last_halo = n // HALO - 1
    row = lambda i: (i, 0)
    tile = pl.BlockSpec((tile_rows, D_MODEL), row)
    return pl.pallas_call(
        functools.partial(_conv_ffn_final_kernel, tiles_per_seq),
        grid=(n // tile_rows,),
        in_specs=[
            tile, tile, tile,
            pl.BlockSpec((HALO, D_MODEL), lambda i: (jnp.maximum(i * halos_per_tile - 1, 0), 0)),
            pl.BlockSpec((HALO, D_MODEL),
                         lambda i: (jnp.minimum((i + 1) * halos_per_tile, last_halo), 0)),
            _resident((CONV_W, D_MODEL)),
            _resident((D_MODEL, D_MODEL)),
            _resident((1, D_MODEL)),
            _resident((D_MODEL, D_FF)),
            _resident((D_FF, D_MODEL)),
            _resident((1, D_MODEL)),
        ],
        out_specs=tile,
        out_shape=jax.ShapeDtypeStruct((n, D_MODEL), F32),
        compiler_params=pltpu.CompilerParams(
            dimension_semantics=("arbitrary",), vmem_limit_bytes=VMEM_LIMIT),
        name="conv_ffn_final",
    )(h, bg, t, t, t, cw, wo, g, w1, w2, fg)


def _rope_tables(seq):
    inv = ROPE_THETA ** (-jnp.arange(0, ROT_DIM, 2, dtype=F32) / ROT_DIM)
    ang = jnp.arange(seq, dtype=F32)[:, None] * inv[None, :]
    cos, sin = jnp.cos(ang), jnp.sin(ang)
    half = ROT_DIM // 2
    lane = jnp.arange(LANES) % B_QK_DIM
    rotated = lane < ROT_DIM
    pick = ((lane % half)[None, :] == jnp.arange(half)[:, None]) & rotated[None, :]
    sign = jnp.where(lane < half, -1.0, 1.0)
    cos_t = jnp.dot(cos, pick.astype(F32), precision=lax.Precision.HIGHEST) + (~rotated).astype(F32)
    sin_t = jnp.dot(sin, pick.astype(F32) * sign, precision=lax.Precision.HIGHEST)
    return cos_t, sin_t


def _trunk(x, p, cos_t, sin_t):
    bsz, seq, _ = x.shape
    h = x.reshape(bsz * seq, D_MODEL)

    out_a, q, k, vt = _mix0_in(h, bsz, seq, p["mix_g"][0], p["e_w_in"], p["a_vnorm_g"], p["a_w_s"],
                               p["a_bias"], cos_t, sin_t)
    lam_init = 0.8 - 0.6 * math.exp(-0.3 * 0)
    out_b = _diff_attn(q, k, vt, p["b_lq1"], p["b_lk1"], p["b_lq2"], p["b_lk2"], p["b_subln_g"],
                       bsz, seq, lam_init)
    h, bg, t = _proj_ffn_conv(h, out_a, out_b, p["e_w_out"], p["ffn_g"][0], p["ffn_w1"][0],
                              p["ffn_w2"][0], p["mix_g"][1], p["c_w_in"])
    h = _conv_ffn_final(h, bg, t, seq, p["c_conv_w"], p["c_w_out"], p["ffn_g"][1], p["ffn_w1"][1],
                        p["ffn_w2"][1], p["final_g"])
    return h.reshape(bsz, seq, D_MODEL)


def kernel(x_prompt, x_sample, norm_mix_g, norm_ffn_g, ffn_w1, ffn_w2, e_w_in, e_w_out, a_vnorm_g, a_w_s, a_b_s, b_lq1, b_lk1, b_lq2, b_lk2, b_subln_g, c_w_in, c_conv_w, c_w_out, final_g):
    depth = norm_mix_g.shape[0]
    p = {
        "mix_g": [norm_mix_g[i].reshape(1, D_MODEL) for i in range(depth)],
        "ffn_g": [norm_ffn_g[i].reshape(1, D_MODEL) for i in range(depth)],
        "ffn_w1": [ffn_w1[i].astype(BF16) for i in range(depth)],
        "ffn_w2": [ffn_w2[i].astype(BF16) for i in range(depth)],
        "e_w_in": e_w_in[0].astype(BF16),
        "e_w_out": e_w_out[0].astype(BF16),
        "a_vnorm_g": a_vnorm_g[0].reshape(1, A_WIDTH),
        "a_w_s": a_w_s[0].astype(BF16),
        "a_bias": jnp.repeat(a_b_s[0].T, A_GROUP_DIM, axis=1),
        "b_lq1": b_lq1[0].reshape(1, B_QK_DIM),
        "b_lk1": b_lk1[0].reshape(1, B_QK_DIM),
        "b_lq2": b_lq2[0].reshape(1, B_QK_DIM),
        "b_lk2": b_lk2[0].reshape(1, B_QK_DIM),
        "b_subln_g": jnp.broadcast_to(b_subln_g[0][:, None], (B_V_DIM, Q_TILE)),
        "c_w_in": c_w_in[0].astype(BF16),
        "c_conv_w": c_conv_w[0],
        "c_w_out": c_w_out[0].astype(BF16),
        "final_g": final_g.reshape(1, D_MODEL),
    }
    cos_t, sin_t = _rope_tables(max(x_prompt.shape[1], x_sample.shape[1]))
    return (_trunk(x_prompt, p, cos_t, sin_t), _trunk(x_sample, p, cos_t, sin_t))
```

```python
import functools
import math

import jax
import jax.numpy as jnp
from jax import lax
from jax.experimental import pallas as pl
from jax.experimental.pallas import tpu as pltpu

D_MODEL = 1024
D_FF = 4 * D_MODEL
EPS = 1e-5
A_WIDTH = 512
A_GROUPS = 4
A_GROUP_DIM = 128
CHUNK = 128
B_HEADS = 4
B_QK_DIM = 64
B_V_DIM = 128
B_WIDTH = 512
ROT_DIM = 16
ROPE_THETA = 500000.0
E_IN = 2 * A_WIDTH + 2 * B_WIDTH + B_WIDTH
CONV_W = 3

LANES = 128
ROW_TILE = 512
MIX_TILE = 1024
SUB_ROWS = 256
CONV_CHAINS = 2
KV_TILE = 512
Q_TILE = 1024
Q_GROUP = 256
KV_STEP = 4
SUM_ROWS = 16
HALO = 16
FF_CHUNK = 1024
VMEM_LIMIT = 56 * 1024 * 1024

F32 = jnp.float32
BF16 = jnp.bfloat16


def _resident(shape):
    return pl.BlockSpec(shape, lambda *_: (0,) * len(shape), pipeline_mode=pl.Buffered(1))


def _resident_layer(shape, layer):
    return pl.BlockSpec((None,) + shape, lambda *_: (layer,) + (0,) * len(shape),
                        pipeline_mode=pl.Buffered(1))


def _rms(x, g):
    ms = jnp.mean(x * x, axis=-1, keepdims=True)
    return (x * lax.rsqrt(ms + EPS)) * g


def _dot(a, b):
    return jnp.dot(a, b, preferred_element_type=F32)


def _reduce_rows(x, op, width=32):
    acc = x[0:width]
    for r in range(width, x.shape[0], width):
        acc = op(acc, x[r:r + width])
    while acc.shape[0] > 8:
        half = acc.shape[0] // 2
        acc = op(acc[:half], acc[half:])
    reduce = jnp.max if op is jnp.maximum else jnp.sum
    return reduce(acc, axis=0, keepdims=True)


def _mix0_in_kernel(x_ref, g_ref, w_ref, vg_ref, ws_ref, bias_ref, cos_ref, sin_ref,
                    oa_ref, q_ref, k_ref, vt_ref):
    lane = lax.broadcasted_iota(jnp.int32, (SUB_ROWS, LANES), 1)
    upper_half = (lane % B_QK_DIM) >= ROT_DIM // 2
    scale = math.log2(math.e) / math.sqrt(B_QK_DIM)
    q0, k0, v0 = 2 * A_WIDTH, 2 * A_WIDTH + B_WIDTH, 2 * A_WIDTH + 2 * B_WIDTH

    pieces = []
    for r0 in range(0, MIX_TILE, SUB_ROWS):
        rows = slice(r0, r0 + SUB_ROWS)
        xn = _rms(x_ref[rows, :], g_ref[...]).astype(BF16)
        pieces.append((rows, [_dot(xn, w_ref[:, lo:hi]) for lo, hi in
                              ((A_WIDTH, q0), (0, A_WIDTH), (q0, k0), (k0, v0), (v0, E_IN))]))

    for rows, (av, au, q, k, v) in pieces:
        u = jax.nn.gelu(au)
        gv = jax.nn.gelu(av)
        for g in range(A_GROUPS):
            cols = slice(g * A_GROUP_DIM, (g + 1) * A_GROUP_DIM)
            vn = _rms(gv[:, cols], vg_ref[:, cols]).astype(BF16)
            for c in range(SUB_ROWS // CHUNK):
                sub = slice(c * CHUNK, (c + 1) * CHUNK)
                mixed = _dot(ws_ref[g], vn[sub, :]) + bias_ref[:, cols]
                oa_ref[rows.start + c * CHUNK:rows.start + (c + 1) * CHUNK, cols] = (
                    u[sub, cols] * mixed).astype(BF16)

        cos = cos_ref[rows, :]
        sin = sin_ref[rows, :]

        def rope(t):
            partner = jnp.where(upper_half, pltpu.roll(t, ROT_DIM // 2, 1),
                                pltpu.roll(t, LANES - ROT_DIM // 2, 1))
            return t * cos + partner * sin

        for h in range(B_HEADS):
            cols = slice(h * LANES, (h + 1) * LANES)
            q_ref[rows, cols] = (rope(q[:, cols]) * scale).astype(BF16)
            k_ref[rows, cols] = rope(k[:, cols]).astype(BF16)

        vt = v.T
        for h in range(B_HEADS):
            lanes = slice(rows.start % KV_TILE, rows.start % KV_TILE + SUB_ROWS)
            vt_ref[0, h, rows.start // KV_TILE, :, lanes] = (
                vt[h * B_V_DIM:(h + 1) * B_V_DIM, :].astype(BF16))


def _mix0_in(x2d, bsz, seq, g, w, vg, ws, bias, cos_t, sin_t):
    n = bsz * seq
    tiles_per_seq = seq // MIX_TILE
    assert MIX_TILE % KV_TILE == 0 and KV_TILE % SUB_ROWS == 0
    row = lambda i: (i, 0)
    return pl.pallas_call(
        _mix0_in_kernel,
        grid=(n // MIX_TILE,),
        in_specs=[
            pl.BlockSpec((MIX_TILE, D_MODEL), row),
            _resident((1, D_MODEL)),
            _resident((D_MODEL, E_IN)),
            _resident((1, A_WIDTH)),
            _resident((A_GROUPS, CHUNK, CHUNK)),
            _resident((CHUNK, A_WIDTH)),
            pl.BlockSpec((MIX_TILE, LANES), lambda i: (i % tiles_per_seq, 0)),
            pl.BlockSpec((MIX_TILE, LANES), lambda i: (i % tiles_per_seq, 0)),
        ],
        out_specs=[
            pl.BlockSpec((MIX_TILE, A_WIDTH), row),
            pl.BlockSpec((MIX_TILE, B_WIDTH), row),
            pl.BlockSpec((MIX_TILE, B_WIDTH), row),
            pl.BlockSpec((1, B_HEADS, MIX_TILE // KV_TILE, B_V_DIM, KV_TILE),
                         lambda i: (i // tiles_per_seq, 0, i % tiles_per_seq, 0, 0)),
        ],
        out_shape=[
            jax.ShapeDtypeStruct((n, A_WIDTH), BF16),
            jax.ShapeDtypeStruct((n, B_WIDTH), BF16),
            jax.ShapeDtypeStruct((n, B_WIDTH), BF16),
            jax.ShapeDtypeStruct((bsz, B_HEADS, seq // KV_TILE, B_V_DIM, KV_TILE), BF16),
        ],
        compiler_params=pltpu.CompilerParams(
            dimension_semantics=("arbitrary",), vmem_limit_bytes=VMEM_LIMIT),
        name="mix0_in",
    )(x2d, g, w, vg, ws, bias, cos_t, sin_t)


def _diff_attn_kernel(lam_init, n_kv, q_tiles, q_ref, k_ref, vt_ref, lq1_ref, lk1_ref, lq2_ref,
                      lk2_ref, sg_ref, o_ref, m_ref, acc_ref, s_ref, mb_ref):
    i = pl.program_id(2)
    units = [(s, c) for s in range(2) for c in range(Q_TILE // Q_GROUP)]
    lane = lax.broadcasted_iota(jnp.int32, (Q_GROUP, LANES), 1)
    stream_lanes = (lane < B_QK_DIM, lane >= B_QK_DIM)

    def stage_scores(slot, qt, jb, u):
        s, c = units[u]
        q_row = pl.multiple_of(qt * Q_TILE + c * Q_GROUP, Q_GROUP)
        qg = q_ref[pl.ds(q_row, Q_GROUP), :]
        qg = jnp.where(stream_lanes[s], qg, jnp.zeros_like(qg))
        kb = k_ref[pl.ds(pl.multiple_of(jb * KV_TILE, KV_TILE), KV_TILE), :]
        st = lax.dot_general(kb, qg, (((1,), (1,)), ((), ())),
                             preferred_element_type=F32)
        s_ref[slot, u] = st
        mb_ref[slot, u] = _reduce_rows(st, jnp.maximum)

    @pl.when(i == 0)
    def _():
        for u in range(len(units)):
            stage_scores(0, 0, 0, u)

    m_ref[...] = jnp.full(m_ref.shape, -jnp.inf, F32)
    acc_ref[...] = jnp.zeros(acc_ref.shape, F32)

    def key_block(jb, slot, prefetch_next):
        vtb = jnp.concatenate([vt_ref[0, 0, jb], jnp.ones((SUM_ROWS, KV_TILE), BF16)], axis=0)
        for u, (s, c) in enumerate(units):
            cols = slice(c * Q_GROUP, (c + 1) * Q_GROUP)
            if prefetch_next:
                stage_scores(1 - slot, i, jb + 1, u)
            m_old = m_ref[s, :, cols]
            m_new = jnp.maximum(m_old, mb_ref[slot, u])
            alpha = jnp.exp2(m_old - m_new)
            p = jnp.exp2(s_ref[slot, u] - m_new)
            acc_ref[s, :, cols] = alpha * acc_ref[s, :, cols] + _dot(vtb, p.astype(BF16))
            m_ref[s, :, cols] = m_new

    def body(t, carry):
        for b in range(KV_STEP):
            key_block(t * KV_STEP + b, b % 2, True)
        return carry

    lax.fori_loop(0, n_kv // KV_STEP - 1, body, 0)
    for b in range(KV_STEP):
        key_block(n_kv - KV_STEP + b, b % 2, b + 1 < KV_STEP)
    for u in range(len(units)):
        stage_scores(0, jnp.minimum(i + 1, q_tiles - 1), 0, u)

    lam = (jnp.exp(jnp.sum(lq1_ref[...] * lk1_ref[...], axis=-1, keepdims=True))
           - jnp.exp(jnp.sum(lq2_ref[...] * lk2_ref[...], axis=-1, keepdims=True)) + lam_init)
    l1 = acc_ref[0, B_V_DIM:B_V_DIM + 1, :]
    l2 = acc_ref[1, B_V_DIM:B_V_DIM + 1, :]
    ot = acc_ref[0, 0:B_V_DIM, :] * (1.0 / l1) - acc_ref[1, 0:B_V_DIM, :] * (lam / l2)
    ms = jnp.mean(ot * ot, axis=0, keepdims=True)
    y = ((ot * lax.rsqrt(ms + EPS)) * sg_ref[...]) * (1.0 - lam_init)
    o_ref[...] = y.T.astype(BF16)


def _diff_attn(q, k, vt, lq1, lk1, lq2, lk2, sg, bsz, seq, lam_init):
    n = bsz * seq
    q_tiles = seq // Q_TILE
    n_kv = seq // KV_TILE
    assert n_kv % KV_STEP == 0 and KV_STEP % 2 == 0
    n_units = 2 * Q_TILE // Q_GROUP
    vec = pl.BlockSpec((1, B_QK_DIM), lambda b, h, i: (0, 0))
    return pl.pallas_call(
        functools.partial(_diff_attn_kernel, lam_init, n_kv, q_tiles),
        grid=(bsz, B_HEADS, q_tiles),
        in_specs=[
            pl.BlockSpec((seq, LANES), lambda b, h, i: (b, h)),
            pl.BlockSpec((seq, LANES), lambda b, h, i: (b, h)),
            pl.BlockSpec((1, 1, n_kv, B_V_DIM, KV_TILE), lambda b, h, i: (b, h, 0, 0, 0)),
            vec, vec, vec, vec,
            pl.BlockSpec((B_V_DIM, Q_TILE), lambda b, h, i: (0, 0)),
        ],
        out_specs=pl.BlockSpec((Q_TILE, LANES), lambda b, h, i: (b * q_tiles + i, h)),
        out_shape=jax.ShapeDtypeStruct((n, B_WIDTH), BF16),
        scratch_shapes=[
            pltpu.VMEM((2, 1, Q_TILE), F32),
            pltpu.VMEM((2, B_V_DIM + SUM_ROWS, Q_TILE), F32),
            pltpu.VMEM((2, n_units, KV_TILE, Q_GROUP), F32),
            pltpu.VMEM((2, n_units, 1, Q_GROUP), F32),
        ],
        compiler_params=pltpu.CompilerParams(
            dimension_semantics=("arbitrary", "arbitrary", "arbitrary"),
            vmem_limit_bytes=VMEM_LIMIT),
        name="diff_attn",
    )(q, k, vt, lq1, lk1, lq2, lk2, sg)


def _ffn_block(h1, g_ref, w1_ref, w2_ref):
    xn = _rms(h1, g_ref[...]).astype(BF16)
    ff = None
    for c in range(D_FF // FF_CHUNK):
        cols = slice(c * FF_CHUNK, (c + 1) * FF_CHUNK)
        t = jnp.square(jnp.maximum(_dot(xn, w1_ref[:, cols]), 0.0)).astype(BF16)
        d = _dot(t, w2_ref[cols, :])
        ff = d if ff is None else ff + d
    return h1 + ff


def _proj_ffn_conv_kernel(h_ref, ma_ref, mb_ref, wo_ref, g_ref, w1_ref, w2_ref, cg_ref, cw_ref,
                          ho_ref, bg_ref, t_ref):
    half = D_MODEL // 2
    h1 = h_ref[...] + _dot(ma_ref[...], wo_ref[0:half, :]) + _dot(mb_ref[...], wo_ref[half:, :])
    h2 = _ffn_block(h1, g_ref, w1_ref, w2_ref)
    ho_ref[...] = h2
    xn = _rms(h2, cg_ref[...]).astype(BF16)
    bg_ref[...] = _dot(xn, cw_ref[:, 0:D_MODEL]).astype(BF16)
    t_ref[...] = (_dot(xn, cw_ref[:, D_MODEL:2 * D_MODEL])
                  * _dot(xn, cw_ref[:, 2 * D_MODEL:])).astype(BF16)


def _proj_ffn_conv(h, mix_a, mix_b, wo, g, w1, w2, cg, cw):
    n = h.shape[0]
    half = D_MODEL // 2
    row = lambda i: (i, 0)
    tile_f32 = pl.BlockSpec((ROW_TILE, D_MODEL), row)
    return pl.pallas_call(
        _proj_ffn_conv_kernel,
        grid=(n // ROW_TILE,),
        in_specs=[
            tile_f32,
            pl.BlockSpec((ROW_TILE, half), row),
            pl.BlockSpec((ROW_TILE, half), row),
            _resident((D_MODEL, D_MODEL)),
            _resident((1, D_MODEL)),
            _resident_layer((D_MODEL, D_FF), 0),
            _resident_layer((D_FF, D_MODEL), 0),
            _resident((1, D_MODEL)),
            _resident((D_MODEL, 3 * D_MODEL)),
        ],
        out_specs=[tile_f32, pl.BlockSpec((ROW_TILE, D_MODEL), row),
                   pl.BlockSpec((ROW_TILE, D_MODEL), row)],
        out_shape=[jax.ShapeDtypeStruct((n, D_MODEL), F32), jax.ShapeDtypeStruct((n, D_MODEL), BF16),
                   jax.ShapeDtypeStruct((n, D_MODEL), BF16)],
        compiler_params=pltpu.CompilerParams(
            dimension_semantics=("arbitrary",), vmem_limit_bytes=VMEM_LIMIT),
        name="proj_ffn_conv",
    )(h, mix_a, mix_b, wo, g, w1, w2, cg, cw)


def _conv_ffn_final_kernel(tiles_per_seq, h_ref, bg_ref, t_ref, tp_ref, tn_ref, cw_ref, wo_ref,
                           g_ref, w1_ref, w2_ref, fg_ref, o_ref):
    i = pl.program_id(0)
    t = t_ref[...].astype(F32)
    before = jnp.where(i % tiles_per_seq == 0, 0.0, tp_ref[HALO - 1:HALO, :].astype(F32))
    after = jnp.where(i % tiles_per_seq == tiles_per_seq - 1, 0.0, tn_ref[0:1, :].astype(F32))
    rows_total = t.shape[0]
    r = lax.broadcasted_iota(jnp.int32, t.shape, 0)
    t_prev = jnp.where(r == 0, before, pltpu.roll(t, 1, 0))
    t_next = jnp.where(r == rows_total - 1, after, pltpu.roll(t, rows_total - 1, 0))
    y = t_prev * cw_ref[0:1, :] + t * cw_ref[1:2, :] + t_next * cw_ref[2:3, :]
    for r0 in range(0, rows_total, ROW_TILE):
        rows = slice(r0, r0 + ROW_TILE)
        mix = (bg_ref[rows, :].astype(F32) * y[rows]).astype(BF16)
        h1 = h_ref[rows, :] + _dot(mix, wo_ref[...])
        o_ref[rows, :] = _rms(_ffn_block(h1, g_ref, w1_ref, w2_ref), fg_ref[...])


def _conv_ffn_final(h, bg, t, seq, cw, wo, g, w1, w2, fg):
    n = h.shape[0]
    tile_rows = CONV_CHAINS * ROW_TILE
    tiles_per_seq = seq // tile_rows
    halos_per_tile = tile_rows // HALO
    last_halo = n // HALO - 1
    row = lambda i: (i, 0)
    tile = pl.BlockSpec((tile_rows, D_MODEL), row)
    return pl.pallas_call(
        functools.partial(_conv_ffn_final_kernel, tiles_per_seq),
        grid=(n // tile_rows,),
        in_specs=[
            tile, tile, tile,
            pl.BlockSpec((HALO, D_MODEL), lambda i: (jnp.maximum(i * halos_per_tile - 1, 0), 0)),
            pl.BlockSpec((HALO, D_MODEL),
                         lambda i: (jnp.minimum((i + 1) * halos_per_tile, last_halo), 0)),
            _resident((CONV_W, D_MODEL)),
            _resident((D_MODEL, D_MODEL)),
            _resident((1, D_MODEL)),
            _resident_layer((D_MODEL, D_FF), 1),
            _resident_layer((D_FF, D_MODEL), 1),
            _resident((1, D_MODEL)),
        ],
        out_specs=tile,
        out_shape=jax.ShapeDtypeStruct((n, D_MODEL), F32),
        compiler_params=pltpu.CompilerParams(
            dimension_semantics=("arbitrary",), vmem_limit_bytes=VMEM_LIMIT),
        name="conv_ffn_final",
    )(h, bg, t, t, t, cw, wo, g, w1, w2, fg)


def _rope_tables(seq):
    inv = ROPE_THETA ** (-jnp.arange(0, ROT_DIM, 2, dtype=F32) / ROT_DIM)
    ang = jnp.arange(seq, dtype=F32)[:, None] * inv[None, :]
    cos, sin = jnp.cos(ang), jnp.sin(ang)
    half = ROT_DIM // 2
    lane = jnp.arange(LANES) % B_QK_DIM
    rotated = lane < ROT_DIM
    pick = ((lane % half)[None, :] == jnp.arange(half)[:, None]) & rotated[None, :]
    sign = jnp.where(lane < half, -1.0, 1.0)
    cos_t = jnp.dot(cos, pick.astype(F32), precision=lax.Precision.HIGHEST) + (~rotated).astype(F32)
    sin_t = jnp.dot(sin, pick.astype(F32) * sign, precision=lax.Precision.HIGHEST)
    return cos_t, sin_t


def _trunk(x, p, cos_t, sin_t):
    bsz, seq, _ = x.shape
    h = x.reshape(bsz * seq, D_MODEL)

    out_a, q, k, vt = _mix0_in(h, bsz, seq, p["mix_g"][0], p["e_w_in"], p["a_vnorm_g"], p["a_w_s"],
                               p["a_bias"], cos_t, sin_t)
    lam_init = 0.8 - 0.6 * math.exp(-0.3 * 0)
    out_b = _diff_attn(q, k, vt, p["b_lq1"], p["b_lk1"], p["b_lq2"], p["b_lk2"], p["b_subln_g"],
                       bsz, seq, lam_init)
    h, bg, t = _proj_ffn_conv(h, out_a, out_b, p["e_w_out"], p["ffn_g"][0], p["ffn_w1"],
                              p["ffn_w2"], p["mix_g"][1], p["c_w_in"])
    h = _conv_ffn_final(h, bg, t, seq, p["c_conv_w"], p["c_w_out"], p["ffn_g"][1], p["ffn_w1"],
                        p["ffn_w2"], p["final_g"])
    return h.reshape(bsz, seq, D_MODEL)


def kernel(x_prompt, x_sample, norm_mix_g, norm_ffn_g, ffn_w1, ffn_w2, e_w_in, e_w_out, a_vnorm_g, a_w_s, a_b_s, b_lq1, b_lk1, b_lq2, b_lk2, b_subln_g, c_w_in, c_conv_w, c_w_out, final_g):
    depth = norm_mix_g.shape[0]
    p = {
        "mix_g": [norm_mix_g[i].reshape(1, D_MODEL) for i in range(depth)],
        "ffn_g": [norm_ffn_g[i].reshape(1, D_MODEL) for i in range(depth)],
        "ffn_w1": ffn_w1.astype(BF16),
        "ffn_w2": ffn_w2.astype(BF16),
        "e_w_in": e_w_in[0].astype(BF16),
        "e_w_out": e_w_out[0].astype(BF16),
        "a_vnorm_g": a_vnorm_g[0].reshape(1, A_WIDTH),
        "a_w_s": a_w_s[0].astype(BF16),
        "a_bias": jnp.repeat(a_b_s[0].T, A_GROUP_DIM, axis=1),
        "b_lq1": b_lq1[0].reshape(1, B_QK_DIM),
        "b_lk1": b_lk1[0].reshape(1, B_QK_DIM),
        "b_lq2": b_lq2[0].reshape(1, B_QK_DIM),
        "b_lk2": b_lk2[0].reshape(1, B_QK_DIM),
        "b_subln_g": jnp.broadcast_to(b_subln_g[0][:, None], (B_V_DIM, Q_TILE)),
        "c_w_in": c_w_in[0].astype(BF16),
        "c_conv_w": c_conv_w[0],
        "c_w_out": c_w_out[0].astype(BF16),
        "final_g": final_g.reshape(1, D_MODEL),
    }
    cos_t, sin_t = _rope_tables(max(x_prompt.shape[1], x_sample.shape[1]))
    return (_trunk(x_prompt, p, cos_t, sin_t), _trunk(x_sample, p, cos_t, sin_t))
```

```python
import functools
import math

import jax
import jax.numpy as jnp
from jax import lax
from jax.experimental import pallas as pl
from jax.experimental.pallas import tpu as pltpu

D_MODEL = 1024
D_FF = 4 * D_MODEL
EPS = 1e-5
A_WIDTH = 512
A_GROUPS = 4
A_GROUP_DIM = 128
CHUNK = 128
B_HEADS = 4
B_QK_DIM = 64
B_V_DIM = 128
B_WIDTH = 512
ROT_DIM = 16
ROPE_THETA = 500000.0
E_IN = 2 * A_WIDTH + 2 * B_WIDTH + B_WIDTH
CONV_W = 3

LANES = 128
ROW_TILE = 512
MIX_TILE = 1024
SUB_ROWS = 256
CONV_CHAINS = 2
KV_TILE = 512
Q_TILE = 1024
Q_GROUP = 256
KV_STEP = 8
SUM_ROWS = 16
HALO = 16
FF_CHUNK = 1024
VMEM_LIMIT = 56 * 1024 * 1024

F32 = jnp.float32
BF16 = jnp.bfloat16


def _resident(shape):
    return pl.BlockSpec(shape, lambda *_: (0,) * len(shape), pipeline_mode=pl.Buffered(1))


def _resident_layer(shape, layer):
    return pl.BlockSpec((None,) + shape, lambda *_: (layer,) + (0,) * len(shape),
                        pipeline_mode=pl.Buffered(1))


def _rms(x, g):
    ms = jnp.mean(x * x, axis=-1, keepdims=True)
    return (x * lax.rsqrt(ms + EPS)) * g


def _dot(a, b):
    return jnp.dot(a, b, preferred_element_type=F32)


def _reduce_rows(x, op, width=32):
    acc = x[0:width]
    for r in range(width, x.shape[0], width):
        acc = op(acc, x[r:r + width])
    while acc.shape[0] > 8:
        half = acc.shape[0] // 2
        acc = op(acc[:half], acc[half:])
    reduce = jnp.max if op is jnp.maximum else jnp.sum
    return reduce(acc, axis=0, keepdims=True)


def _mix0_in_kernel(x_ref, g_ref, w_ref, vg_ref, ws_ref, bias_ref, cos_ref, sin_ref,
                    oa_ref, q_ref, k_ref, vt_ref):
    lane = lax.broadcasted_iota(jnp.int32, (SUB_ROWS, LANES), 1)
    upper_half = (lane % B_QK_DIM) >= ROT_DIM // 2
    scale = math.log2(math.e) / math.sqrt(B_QK_DIM)
    q0, k0, v0 = 2 * A_WIDTH, 2 * A_WIDTH + B_WIDTH, 2 * A_WIDTH + 2 * B_WIDTH

    pieces = []
    for r0 in range(0, MIX_TILE, SUB_ROWS):
        rows = slice(r0, r0 + SUB_ROWS)
        xn = _rms(x_ref[rows, :], g_ref[...]).astype(BF16)
        pieces.append((rows, [_dot(xn, w_ref[:, lo:hi]) for lo, hi in
                              ((A_WIDTH, q0), (0, A_WIDTH), (q0, k0), (k0, v0), (v0, E_IN))]))

    for rows, (av, au, q, k, v) in pieces:
        u = jax.nn.gelu(au)
        gv = jax.nn.gelu(av)
        for g in range(A_GROUPS):
            cols = slice(g * A_GROUP_DIM, (g + 1) * A_GROUP_DIM)
            vn = _rms(gv[:, cols], vg_ref[:, cols]).astype(BF16)
            for c in range(SUB_ROWS // CHUNK):
                sub = slice(c * CHUNK, (c + 1) * CHUNK)
                mixed = _dot(ws_ref[g], vn[sub, :]) + bias_ref[:, cols]
                oa_ref[rows.start + c * CHUNK:rows.start + (c + 1) * CHUNK, cols] = (
                    u[sub, cols] * mixed).astype(BF16)

        cos = cos_ref[rows, :]
        sin = sin_ref[rows, :]

        def rope(t):
            partner = jnp.where(upper_half, pltpu.roll(t, ROT_DIM // 2, 1),
                                pltpu.roll(t, LANES - ROT_DIM // 2, 1))
            return t * cos + partner * sin

        for h in range(B_HEADS):
            cols = slice(h * LANES, (h + 1) * LANES)
            q_ref[rows, cols] = (rope(q[:, cols]) * scale).astype(BF16)
            k_ref[rows, cols] = rope(k[:, cols]).astype(BF16)

        vt = v.T
        for h in range(B_HEADS):
            lanes = slice(rows.start % KV_TILE, rows.start % KV_TILE + SUB_ROWS)
            vt_ref[0, h, rows.start // KV_TILE, :, lanes] = (
                vt[h * B_V_DIM:(h + 1) * B_V_DIM, :].astype(BF16))


def _mix0_in(x2d, bsz, seq, g, w, vg, ws, bias, cos_t, sin_t):
    n = bsz * seq
    tiles_per_seq = seq // MIX_TILE
    assert MIX_TILE % KV_TILE == 0 and KV_TILE % SUB_ROWS == 0
    row = lambda i: (i, 0)
    return pl.pallas_call(
        _mix0_in_kernel,
        grid=(n // MIX_TILE,),
        in_specs=[
            pl.BlockSpec((MIX_TILE, D_MODEL), row),
            _resident((1, D_MODEL)),
            _resident((D_MODEL, E_IN)),
            _resident((1, A_WIDTH)),
            _resident((A_GROUPS, CHUNK, CHUNK)),
            _resident((CHUNK, A_WIDTH)),
            pl.BlockSpec((MIX_TILE, LANES), lambda i: (i % tiles_per_seq, 0)),
            pl.BlockSpec((MIX_TILE, LANES), lambda i: (i % tiles_per_seq, 0)),
        ],
        out_specs=[
            pl.BlockSpec((MIX_TILE, A_WIDTH), row),
            pl.BlockSpec((MIX_TILE, B_WIDTH), row),
            pl.BlockSpec((MIX_TILE, B_WIDTH), row),
            pl.BlockSpec((1, B_HEADS, MIX_TILE // KV_TILE, B_V_DIM, KV_TILE),
                         lambda i: (i // tiles_per_seq, 0, i % tiles_per_seq, 0, 0)),
        ],
        out_shape=[
            jax.ShapeDtypeStruct((n, A_WIDTH), BF16),
            jax.ShapeDtypeStruct((n, B_WIDTH), BF16),
            jax.ShapeDtypeStruct((n, B_WIDTH), BF16),
            jax.ShapeDtypeStruct((bsz, B_HEADS, seq // KV_TILE, B_V_DIM, KV_TILE), BF16),
        ],
        compiler_params=pltpu.CompilerParams(
            dimension_semantics=("arbitrary",), vmem_limit_bytes=VMEM_LIMIT),
        name="mix0_in",
    )(x2d, g, w, vg, ws, bias, cos_t, sin_t)


def _diff_attn_kernel(lam_init, n_kv, q_tiles, q_ref, k_ref, vt_ref, lq1_ref, lk1_ref, lq2_ref,
                      lk2_ref, sg_ref, o_ref, m_ref, acc_ref, s_ref, mb_ref):
    i = pl.program_id(2)
    units = [(s, c) for s in range(2) for c in range(Q_TILE // Q_GROUP)]
    lane = lax.broadcasted_iota(jnp.int32, (Q_GROUP, LANES), 1)
    stream_lanes = (lane < B_QK_DIM, lane >= B_QK_DIM)

    def stage_scores(slot, qt, jb, u):
        s, c = units[u]
        q_row = pl.multiple_of(qt * Q_TILE + c * Q_GROUP, Q_GROUP)
        qg = q_ref[pl.ds(q_row, Q_GROUP), :]
        qg = jnp.where(stream_lanes[s], qg, jnp.zeros_like(qg))
        kb = k_ref[pl.ds(pl.multiple_of(jb * KV_TILE, KV_TILE), KV_TILE), :]
        st = lax.dot_general(kb, qg, (((1,), (1,)), ((), ())),
                             preferred_element_type=F32)
        s_ref[slot, u] = st
        mb_ref[slot, u] = _reduce_rows(st, jnp.maximum)

    @pl.when(i == 0)
    def _():
        for u in range(len(units)):
            stage_scores(0, 0, 0, u)

    m_ref[...] = jnp.full(m_ref.shape, -jnp.inf, F32)
    acc_ref[...] = jnp.zeros(acc_ref.shape, F32)

    def key_block(jb, slot, prefetch_next):
        vtb = jnp.concatenate([vt_ref[0, 0, jb], jnp.ones((SUM_ROWS, KV_TILE), BF16)], axis=0)
        for u, (s, c) in enumerate(units):
            cols = slice(c * Q_GROUP, (c + 1) * Q_GROUP)
            if prefetch_next:
                stage_scores(1 - slot, i, jb + 1, u)
            m_old = m_ref[s, :, cols]
            m_new = jnp.maximum(m_old, mb_ref[slot, u])
            alpha = jnp.exp2(m_old - m_new)
            p = jnp.exp2(s_ref[slot, u] - m_new)
            acc_ref[s, :, cols] = alpha * acc_ref[s, :, cols] + _dot(vtb, p.astype(BF16))
            m_ref[s, :, cols] = m_new

    def body(t, carry):
        for b in range(KV_STEP):
            key_block(t * KV_STEP + b, b % 2, True)
        return carry

    lax.fori_loop(0, n_kv // KV_STEP - 1, body, 0)
    for b in range(KV_STEP):
        key_block(n_kv - KV_STEP + b, b % 2, b + 1 < KV_STEP)
    for u in range(len(units)):
        stage_scores(0, jnp.minimum(i + 1, q_tiles - 1), 0, u)

    lam = (jnp.exp(jnp.sum(lq1_ref[...] * lk1_ref[...], axis=-1, keepdims=True))
           - jnp.exp(jnp.sum(lq2_ref[...] * lk2_ref[...], axis=-1, keepdims=True)) + lam_init)
    l1 = acc_ref[0, B_V_DIM:B_V_DIM + 1, :]
    l2 = acc_ref[1, B_V_DIM:B_V_DIM + 1, :]
    ot = acc_ref[0, 0:B_V_DIM, :] * (1.0 / l1) - acc_ref[1, 0:B_V_DIM, :] * (lam / l2)
    ms = jnp.mean(ot * ot, axis=0, keepdims=True)
    y = ((ot * lax.rsqrt(ms + EPS)) * sg_ref[...]) * (1.0 - lam_init)
    o_ref[...] = y.T.astype(BF16)


def _diff_attn(q, k, vt, lq1, lk1, lq2, lk2, sg, bsz, seq, lam_init):
    n = bsz * seq
    q_tiles = seq // Q_TILE
    n_kv = seq // KV_TILE
    assert n_kv % KV_STEP == 0 and KV_STEP % 2 == 0
    n_units = 2 * Q_TILE // Q_GROUP
    vec = pl.BlockSpec((1, B_QK_DIM), lambda b, h, i: (0, 0))
    return pl.pallas_call(
        functools.partial(_diff_attn_kernel, lam_init, n_kv, q_tiles),
        grid=(bsz, B_HEADS, q_tiles),
        in_specs=[
            pl.BlockSpec((seq, LANES), lambda b, h, i: (b, h)),
            pl.BlockSpec((seq, LANES), lambda b, h, i: (b, h)),
            pl.BlockSpec((1, 1, n_kv, B_V_DIM, KV_TILE), lambda b, h, i: (b, h, 0, 0, 0)),
            vec, vec, vec, vec,
            pl.BlockSpec((B_V_DIM, Q_TILE), lambda b, h, i: (0, 0)),
        ],
        out_specs=pl.BlockSpec((Q_TILE, LANES), lambda b, h, i: (b * q_tiles + i, h)),
        out_shape=jax.ShapeDtypeStruct((n, B_WIDTH), BF16),
        scratch_shapes=[
            pltpu.VMEM((2, 1, Q_TILE), F32),
            pltpu.VMEM((2, B_V_DIM + SUM_ROWS, Q_TILE), F32),
            pltpu.VMEM((2, n_units, KV_TILE, Q_GROUP), F32),
            pltpu.VMEM((2, n_units, 1, Q_GROUP), F32),
        ],
        compiler_params=pltpu.CompilerParams(
            dimension_semantics=("arbitrary", "arbitrary", "arbitrary"),
            vmem_limit_bytes=VMEM_LIMIT),
        name="diff_attn",
    )(q, k, vt, lq1, lk1, lq2, lk2, sg)


def _ffn_block(h1, g_ref, w1_ref, w2_ref):
    xn = _rms(h1, g_ref[...]).astype(BF16)
    ff = None
    for c in range(D_FF // FF_CHUNK):
        cols = slice(c * FF_CHUNK, (c + 1) * FF_CHUNK)
        t = jnp.square(jnp.maximum(_dot(xn, w1_ref[:, cols]), 0.0)).astype(BF16)
        d = _dot(t, w2_ref[cols, :])
        ff = d if ff is None else ff + d
    return h1 + ff


def _proj_ffn_conv_kernel(h_ref, ma_ref, mb_ref, wo_ref, g_ref, w1_ref, w2_ref, cg_ref, cw_ref,
                          ho_ref, bg_ref, t_ref):
    half = D_MODEL // 2
    h1 = h_ref[...] + _dot(ma_ref[...], wo_ref[0:half, :]) + _dot(mb_ref[...], wo_ref[half:, :])
    h2 = _ffn_block(h1, g_ref, w1_ref, w2_ref)
    ho_ref[...] = h2
    xn = _rms(h2, cg_ref[...]).astype(BF16)
    bg_ref[...] = _dot(xn, cw_ref[:, 0:D_MODEL]).astype(BF16)
    t_ref[...] = (_dot(xn, cw_ref[:, D_MODEL:2 * D_MODEL])
                  * _dot(xn, cw_ref[:, 2 * D_MODEL:])).astype(BF16)


def _proj_ffn_conv(h, mix_a, mix_b, wo, g, w1, w2, cg, cw):
    n = h.shape[0]
    half = D_MODEL // 2
    row = lambda i: (i, 0)
    tile_f32 = pl.BlockSpec((ROW_TILE, D_MODEL), row)
    return pl.pallas_call(
        _proj_ffn_conv_kernel,
        grid=(n // ROW_TILE,),
        in_specs=[
            tile_f32,
            pl.BlockSpec((ROW_TILE, half), row),
            pl.BlockSpec((ROW_TILE, half), row),
            _resident((D_MODEL, D_MODEL)),
            _resident((1, D_MODEL)),
            _resident_layer((D_MODEL, D_FF), 0),
            _resident_layer((D_FF, D_MODEL), 0),
            _resident((1, D_MODEL)),
            _resident((D_MODEL, 3 * D_MODEL)),
        ],
        out_specs=[tile_f32, pl.BlockSpec((ROW_TILE, D_MODEL), row),
                   pl.BlockSpec((ROW_TILE, D_MODEL), row)],
        out_shape=[jax.ShapeDtypeStruct((n, D_MODEL), F32), jax.ShapeDtypeStruct((n, D_MODEL), BF16),
                   jax.ShapeDtypeStruct((n, D_MODEL), BF16)],
        compiler_params=pltpu.CompilerParams(
            dimension_semantics=("arbitrary",), vmem_limit_bytes=VMEM_LIMIT),
        name="proj_ffn_conv",
    )(h, mix_a, mix_b, wo, g, w1, w2, cg, cw)


def _conv_ffn_final_kernel(tiles_per_seq, h_ref, bg_ref, t_ref, tp_ref, tn_ref, cw_ref, wo_ref,
                           g_ref, w1_ref, w2_ref, fg_ref, o_ref):
    i = pl.program_id(0)
    t = t_ref[...].astype(F32)
    before = jnp.where(i % tiles_per_seq == 0, 0.0, tp_ref[HALO - 1:HALO, :].astype(F32))
    after = jnp.where(i % tiles_per_seq == tiles_per_seq - 1, 0.0, tn_ref[0:1, :].astype(F32))
    rows_total = t.shape[0]
    r = lax.broadcasted_iota(jnp.int32, t.shape, 0)
    t_prev = jnp.where(r == 0, before, pltpu.roll(t, 1, 0))
    t_next = jnp.where(r == rows_total - 1, after, pltpu.roll(t, rows_total - 1, 0))
    y = t_prev * cw_ref[0:1, :] + t * cw_ref[1:2, :] + t_next * cw_ref[2:3, :]
    for r0 in range(0, rows_total, ROW_TILE):
        rows = slice(r0, r0 + ROW_TILE)
        mix = (bg_ref[rows, :].astype(F32) * y[rows]).astype(BF16)
        h1 = h_ref[rows, :] + _dot(mix, wo_ref[...])
        o_ref[rows, :] = _rms(_ffn_block(h1, g_ref, w1_ref, w2_ref), fg_ref[...])


def _conv_ffn_final(h, bg, t, seq, cw, wo, g, w1, w2, fg):
    n = h.shape[0]
    tile_rows = CONV_CHAINS * ROW_TILE
    tiles_per_seq = seq // tile_rows
    halos_per_tile = tile_rows // HALO
    last_halo = n // HALO - 1
    row = lambda i: (i, 0)
    tile = pl.BlockSpec((tile_rows, D_MODEL), row)
    return pl.pallas_call(
        functools.partial(_conv_ffn_final_kernel, tiles_per_seq),
        grid=(n // tile_rows,),
        in_specs=[
            tile, tile, tile,
            pl.BlockSpec((HALO, D_MODEL), lambda i: (jnp.maximum(i * halos_per_tile - 1, 0), 0)),
            pl.BlockSpec((HALO, D_MODEL),
                         lambda i: (jnp.minimum((i + 1) * halos_per_tile, last_halo), 0)),
            _resident((CONV_W, D_MODEL)),
            _resident((D_MODEL, D_MODEL)),
            _resident((1, D_MODEL)),
            _resident_layer((D_MODEL, D_FF), 1),
            _resident_layer((D_FF, D_MODEL), 1),
            _resident((1, D_MODEL)),
        ],
        out_specs=tile,
        out_shape=jax.ShapeDtypeStruct((n, D_MODEL), F32),
        compiler_params=pltpu.CompilerParams(
            dimension_semantics=("arbitrary",), vmem_limit_bytes=VMEM_LIMIT),
        name="conv_ffn_final",
    )(h, bg, t, t, t, cw, wo, g, w1, w2, fg)


def _rope_tables(seq):
    inv = ROPE_THETA ** (-jnp.arange(0, ROT_DIM, 2, dtype=F32) / ROT_DIM)
    ang = jnp.arange(seq, dtype=F32)[:, None] * inv[None, :]
    cos, sin = jnp.cos(ang), jnp.sin(ang)
    half = ROT_DIM // 2
    lane = jnp.arange(LANES) % B_QK_DIM
    rotated = lane < ROT_DIM
    pick = ((lane % half)[None, :] == jnp.arange(half)[:, None]) & rotated[None, :]
    sign = jnp.where(lane < half, -1.0, 1.0)
    cos_t = jnp.dot(cos, pick.astype(F32), precision=lax.Precision.HIGHEST) + (~rotated).astype(F32)
    sin_t = jnp.dot(sin, pick.astype(F32) * sign, precision=lax.Precision.HIGHEST)
    return cos_t, sin_t


def _trunk(x, p, cos_t, sin_t):
    bsz, seq, _ = x.shape
    h = x.reshape(bsz * seq, D_MODEL)

    out_a, q, k, vt = _mix0_in(h, bsz, seq, p["mix_g"][0], p["e_w_in"], p["a_vnorm_g"], p["a_w_s"],
                               p["a_bias"], cos_t, sin_t)
    lam_init = 0.8 - 0.6 * math.exp(-0.3 * 0)
    out_b = _diff_attn(q, k, vt, p["b_lq1"], p["b_lk1"], p["b_lq2"], p["b_lk2"], p["b_subln_g"],
                       bsz, seq, lam_init)
    h, bg, t = _proj_ffn_conv(h, out_a, out_b, p["e_w_out"], p["ffn_g"][0], p["ffn_w1"],
                              p["ffn_w2"], p["mix_g"][1], p["c_w_in"])
    h = _conv_ffn_final(h, bg, t, seq, p["c_conv_w"], p["c_w_out"], p["ffn_g"][1], p["ffn_w1"],
                        p["ffn_w2"], p["final_g"])
    return h.reshape(bsz, seq, D_MODEL)


def kernel(x_prompt, x_sample, norm_mix_g, norm_ffn_g, ffn_w1, ffn_w2, e_w_in, e_w_out, a_vnorm_g, a_w_s, a_b_s, b_lq1, b_lk1, b_lq2, b_lk2, b_subln_g, c_w_in, c_conv_w, c_w_out, final_g):
    depth = norm_mix_g.shape[0]
    p = {
        "mix_g": [norm_mix_g[i].reshape(1, D_MODEL) for i in range(depth)],
        "ffn_g": [norm_ffn_g[i].reshape(1, D_MODEL) for i in range(depth)],
        "ffn_w1": ffn_w1.astype(BF16),
        "ffn_w2": ffn_w2.astype(BF16),
        "e_w_in": e_w_in[0].astype(BF16),
        "e_w_out": e_w_out[0].astype(BF16),
        "a_vnorm_g": a_vnorm_g[0].reshape(1, A_WIDTH),
        "a_w_s": a_w_s[0].astype(BF16),
        "a_bias": jnp.repeat(a_b_s[0].T, A_GROUP_DIM, axis=1),
        "b_lq1": b_lq1[0].reshape(1, B_QK_DIM),
        "b_lk1": b_lk1[0].reshape(1, B_QK_DIM),
        "b_lq2": b_lq2[0].reshape(1, B_QK_DIM),
        "b_lk2": b_lk2[0].reshape(1, B_QK_DIM),
        "b_subln_g": jnp.broadcast_to(b_subln_g[0][:, None], (B_V_DIM, Q_TILE)),
        "c_w_in": c_w_in[0].astype(BF16),
        "c_conv_w": c_conv_w[0],
        "c_w_out": c_w_out[0].astype(BF16),
        "final_g": final_g.reshape(1, D_MODEL),
    }
    cos_t, sin_t = _rope_tables(max(x_prompt.shape[1], x_sample.shape[1]))
    return (_trunk(x_prompt, p, cos_t, sin_t), _trunk(x_sample, p, cos_t, sin_t))
```

```python
import functools
import math

import jax
import jax.numpy as jnp
from jax import lax
from jax.experimental import pallas as pl
from jax.experimental.pallas import tpu as pltpu

D_MODEL = 1024
D_FF = 4 * D_MODEL
EPS = 1e-5
A_WIDTH = 512
A_GROUPS = 4
A_GROUP_DIM = 128
CHUNK = 128
B_HEADS = 4
B_QK_DIM = 64
B_V_DIM = 128
B_WIDTH = 512
ROT_DIM = 16
ROPE_THETA = 500000.0
E_IN = 2 * A_WIDTH + 2 * B_WIDTH + B_WIDTH
CONV_W = 3

LANES = 128
ROW_TILE = 512
MIX_TILE = 1024
SUB_ROWS = 256
CONV_CHAINS = 2
KV_TILE = 512
Q_TILE = 1024
Q_GROUP = 256
SUM_ROWS = 16
HALO = 16
FF_CHUNK = 1024
VMEM_LIMIT = 56 * 1024 * 1024

F32 = jnp.float32
BF16 = jnp.bfloat16


def _resident(shape):
    return pl.BlockSpec(shape, lambda *_: (0,) * len(shape), pipeline_mode=pl.Buffered(1))


def _resident_layer(shape, layer):
    return pl.BlockSpec((None,) + shape, lambda *_: (layer,) + (0,) * len(shape),
                        pipeline_mode=pl.Buffered(1))


def _rms(x, g):
    ms = jnp.mean(x * x, axis=-1, keepdims=True)
    return (x * lax.rsqrt(ms + EPS)) * g


def _dot(a, b):
    return jnp.dot(a, b, preferred_element_type=F32)


def _reduce_rows(x, op, width=32):
    acc = x[0:width]
    for r in range(width, x.shape[0], width):
        acc = op(acc, x[r:r + width])
    while acc.shape[0] > 8:
        half = acc.shape[0] // 2
        acc = op(acc[:half], acc[half:])
    reduce = jnp.max if op is jnp.maximum else jnp.sum
    return reduce(acc, axis=0, keepdims=True)


def _mix0_in_kernel(x_ref, g_ref, w_ref, vg_ref, ws_ref, bias_ref, cos_ref, sin_ref,
                    oa_ref, q_ref, k_ref, vt_ref):
    lane = lax.broadcasted_iota(jnp.int32, (SUB_ROWS, LANES), 1)
    upper_half = (lane % B_QK_DIM) >= ROT_DIM // 2
    scale = math.log2(math.e) / math.sqrt(B_QK_DIM)
    q0, k0, v0 = 2 * A_WIDTH, 2 * A_WIDTH + B_WIDTH, 2 * A_WIDTH + 2 * B_WIDTH

    pieces = []
    for r0 in range(0, MIX_TILE, SUB_ROWS):
        rows = slice(r0, r0 + SUB_ROWS)
        xn = _rms(x_ref[rows, :], g_ref[...]).astype(BF16)
        pieces.append((rows, [_dot(xn, w_ref[:, lo:hi]) for lo, hi in
                              ((A_WIDTH, q0), (0, A_WIDTH), (q0, k0), (k0, v0), (v0, E_IN))]))

    for rows, (av, au, q, k, v) in pieces:
        u = jax.nn.gelu(au)
        gv = jax.nn.gelu(av)
        for g in range(A_GROUPS):
            cols = slice(g * A_GROUP_DIM, (g + 1) * A_GROUP_DIM)
            vn = _rms(gv[:, cols], vg_ref[:, cols]).astype(BF16)
            for c in range(SUB_ROWS // CHUNK):
                sub = slice(c * CHUNK, (c + 1) * CHUNK)
                mixed = _dot(ws_ref[g], vn[sub, :]) + bias_ref[:, cols]
                oa_ref[rows.start + c * CHUNK:rows.start + (c + 1) * CHUNK, cols] = (
                    u[sub, cols] * mixed).astype(BF16)

        cos = cos_ref[rows, :]
        sin = sin_ref[rows, :]

        def rope(t):
            partner = jnp.where(upper_half, pltpu.roll(t, ROT_DIM // 2, 1),
                                pltpu.roll(t, LANES - ROT_DIM // 2, 1))
            return t * cos + partner * sin

        for h in range(B_HEADS):
            cols = slice(h * LANES, (h + 1) * LANES)
            q_ref[rows, cols] = (rope(q[:, cols]) * scale).astype(BF16)
            k_ref[rows, cols] = rope(k[:, cols]).astype(BF16)

        vt = v.T
        for h in range(B_HEADS):
            lanes = slice(rows.start % KV_TILE, rows.start % KV_TILE + SUB_ROWS)
            vt_ref[0, h, rows.start // KV_TILE, :, lanes] = (
                vt[h * B_V_DIM:(h + 1) * B_V_DIM, :].astype(BF16))


def _mix0_in(x2d, bsz, seq, g, w, vg, ws, bias, cos_t, sin_t):
    n = bsz * seq
    tiles_per_seq = seq // MIX_TILE
    assert MIX_TILE % KV_TILE == 0 and KV_TILE % SUB_ROWS == 0
    row = lambda i: (i, 0)
    return pl.pallas_call(
        _mix0_in_kernel,
        grid=(n // MIX_TILE,),
        in_specs=[
            pl.BlockSpec((MIX_TILE, D_MODEL), row),
            _resident((1, D_MODEL)),
            _resident((D_MODEL, E_IN)),
            _resident((1, A_WIDTH)),
            _resident((A_GROUPS, CHUNK, CHUNK)),
            _resident((CHUNK, A_WIDTH)),
            pl.BlockSpec((MIX_TILE, LANES), lambda i: (i % tiles_per_seq, 0)),
            pl.BlockSpec((MIX_TILE, LANES), lambda i: (i % tiles_per_seq, 0)),
        ],
        out_specs=[
            pl.BlockSpec((MIX_TILE, A_WIDTH), row),
            pl.BlockSpec((MIX_TILE, B_WIDTH), row),
            pl.BlockSpec((MIX_TILE, B_WIDTH), row),
            pl.BlockSpec((1, B_HEADS, MIX_TILE // KV_TILE, B_V_DIM, KV_TILE),
                         lambda i: (i // tiles_per_seq, 0, i % tiles_per_seq, 0, 0)),
        ],
        out_shape=[
            jax.ShapeDtypeStruct((n, A_WIDTH), BF16),
            jax.ShapeDtypeStruct((n, B_WIDTH), BF16),
            jax.ShapeDtypeStruct((n, B_WIDTH), BF16),
            jax.ShapeDtypeStruct((bsz, B_HEADS, seq // KV_TILE, B_V_DIM, KV_TILE), BF16),
        ],
        compiler_params=pltpu.CompilerParams(
            dimension_semantics=("arbitrary",), vmem_limit_bytes=VMEM_LIMIT),
        name="mix0_in",
    )(x2d, g, w, vg, ws, bias, cos_t, sin_t)


def _diff_attn_kernel(lam_init, n_kv, q_tiles, q_ref, k_ref, vt_ref, lq1_ref, lk1_ref, lq2_ref,
                      lk2_ref, sg_ref, o_ref, m_ref, acc_ref, s_ref, mb_ref):
    i = pl.program_id(2)
    units = [(s, c) for s in range(2) for c in range(Q_TILE // Q_GROUP)]
    lane = lax.broadcasted_iota(jnp.int32, (Q_GROUP, LANES), 1)
    stream_lanes = (lane < B_QK_DIM, lane >= B_QK_DIM)

    def stage_scores(slot, qt, jb, u):
        s, c = units[u]
        q_row = pl.multiple_of(qt * Q_TILE + c * Q_GROUP, Q_GROUP)
        qg = q_ref[pl.ds(q_row, Q_GROUP), :]
        qg = jnp.where(stream_lanes[s], qg, jnp.zeros_like(qg))
        kb = k_ref[pl.ds(pl.multiple_of(jb * KV_TILE, KV_TILE), KV_TILE), :]
        st = lax.dot_general(kb, qg, (((1,), (1,)), ((), ())),
                             preferred_element_type=F32)
        s_ref[slot, u] = st
        mb_ref[slot, u] = _reduce_rows(st, jnp.maximum)

    @pl.when(i == 0)
    def _():
        for u in range(len(units)):
            stage_scores(0, 0, 0, u)

    m_ref[...] = jnp.full(m_ref.shape, -jnp.inf, F32)
    acc_ref[...] = jnp.zeros(acc_ref.shape, F32)

    def key_block(jb, slot, prefetch_next):
        vtb = jnp.concatenate([vt_ref[0, 0, jb], jnp.ones((SUM_ROWS, KV_TILE), BF16)], axis=0)
        for u, (s, c) in enumerate(units):
            cols = slice(c * Q_GROUP, (c + 1) * Q_GROUP)
            if prefetch_next:
                stage_scores(1 - slot, i, jb + 1, u)
            m_old = m_ref[s, :, cols]
            m_new = jnp.maximum(m_old, mb_ref[slot, u])
            alpha = jnp.exp2(m_old - m_new)
            p = jnp.exp2(s_ref[slot, u] - m_new)
            acc_ref[s, :, cols] = alpha * acc_ref[s, :, cols] + _dot(vtb, p.astype(BF16))
            m_ref[s, :, cols] = m_new

    for jb in range(n_kv):
        key_block(jb, jb % 2, jb + 1 < n_kv)
    for u in range(len(units)):
        stage_scores(0, jnp.minimum(i + 1, q_tiles - 1), 0, u)

    lam = (jnp.exp(jnp.sum(lq1_ref[...] * lk1_ref[...], axis=-1, keepdims=True))
           - jnp.exp(jnp.sum(lq2_ref[...] * lk2_ref[...], axis=-1, keepdims=True)) + lam_init)
    l1 = acc_ref[0, B_V_DIM:B_V_DIM + 1, :]
    l2 = acc_ref[1, B_V_DIM:B_V_DIM + 1, :]
    ot = acc_ref[0, 0:B_V_DIM, :] * (1.0 / l1) - acc_ref[1, 0:B_V_DIM, :] * (lam / l2)
    ms = jnp.mean(ot * ot, axis=0, keepdims=True)
    y = ((ot * lax.rsqrt(ms + EPS)) * sg_ref[...]) * (1.0 - lam_init)
    o_ref[...] = y.T.astype(BF16)


def _diff_attn(q, k, vt, lq1, lk1, lq2, lk2, sg, bsz, seq, lam_init):
    n = bsz * seq
    q_tiles = seq // Q_TILE
    n_kv = seq // KV_TILE
    n_units = 2 * Q_TILE // Q_GROUP
    vec = pl.BlockSpec((1, B_QK_DIM), lambda b, h, i: (0, 0))
    return pl.pallas_call(
        functools.partial(_diff_attn_kernel, lam_init, n_kv, q_tiles),
        grid=(bsz, B_HEADS, q_tiles),
        in_specs=[
            pl.BlockSpec((seq, LANES), lambda b, h, i: (b, h)),
            pl.BlockSpec((seq, LANES), lambda b, h, i: (b, h)),
            pl.BlockSpec((1, 1, n_kv, B_V_DIM, KV_TILE), lambda b, h, i: (b, h, 0, 0, 0)),
            vec, vec, vec, vec,
            pl.BlockSpec((B_V_DIM, Q_TILE), lambda b, h, i: (0, 0)),
        ],
        out_specs=pl.BlockSpec((Q_TILE, LANES), lambda b, h, i: (b * q_tiles + i, h)),
        out_shape=jax.ShapeDtypeStruct((n, B_WIDTH), BF16),
        scratch_shapes=[
            pltpu.VMEM((2, 1, Q_TILE), F32),
            pltpu.VMEM((2, B_V_DIM + SUM_ROWS, Q_TILE), F32),
            pltpu.VMEM((2, n_units, KV_TILE, Q_GROUP), F32),
            pltpu.VMEM((2, n_units, 1, Q_GROUP), F32),
        ],
        compiler_params=pltpu.CompilerParams(
            dimension_semantics=("arbitrary", "arbitrary", "arbitrary"),
            vmem_limit_bytes=VMEM_LIMIT),
        name="diff_attn",
    )(q, k, vt, lq1, lk1, lq2, lk2, sg)


def _ffn_block(h1, g_ref, w1_ref, w2_ref):
    xn = _rms(h1, g_ref[...]).astype(BF16)
    ff = None
    for c in range(D_FF // FF_CHUNK):
        cols = slice(c * FF_CHUNK, (c + 1) * FF_CHUNK)
        t = jnp.square(jnp.maximum(_dot(xn, w1_ref[:, cols]), 0.0)).astype(BF16)
        d = _dot(t, w2_ref[cols, :])
        ff = d if ff is None else ff + d
    return h1 + ff


def _proj_ffn_conv_kernel(h_ref, ma_ref, mb_ref, wo_ref, g_ref, w1_ref, w2_ref, cg_ref, cw_ref,
                          ho_ref, bg_ref, t_ref):
    half = D_MODEL // 2
    h1 = h_ref[...] + _dot(ma_ref[...], wo_ref[0:half, :]) + _dot(mb_ref[...], wo_ref[half:, :])
    h2 = _ffn_block(h1, g_ref, w1_ref, w2_ref)
    ho_ref[...] = h2
    xn = _rms(h2, cg_ref[...]).astype(BF16)
    bg_ref[...] = _dot(xn, cw_ref[:, 0:D_MODEL]).astype(BF16)
    t_ref[...] = (_dot(xn, cw_ref[:, D_MODEL:2 * D_MODEL])
                  * _dot(xn, cw_ref[:, 2 * D_MODEL:])).astype(BF16)


def _proj_ffn_conv(h, mix_a, mix_b, wo, g, w1, w2, cg, cw):
    n = h.shape[0]
    half = D_MODEL // 2
    row = lambda i: (i, 0)
    tile_f32 = pl.BlockSpec((ROW_TILE, D_MODEL), row)
    return pl.pallas_call(
        _proj_ffn_conv_kernel,
        grid=(n // ROW_TILE,),
        in_specs=[
            tile_f32,
            pl.BlockSpec((ROW_TILE, half), row),
            pl.BlockSpec((ROW_TILE, half), row),
            _resident((D_MODEL, D_MODEL)),
            _resident((1, D_MODEL)),
            _resident_layer((D_MODEL, D_FF), 0),
            _resident_layer((D_FF, D_MODEL), 0),
            _resident((1, D_MODEL)),
            _resident((D_MODEL, 3 * D_MODEL)),
        ],
        out_specs=[tile_f32, pl.BlockSpec((ROW_TILE, D_MODEL), row),
                   pl.BlockSpec((ROW_TILE, D_MODEL), row)],
        out_shape=[jax.ShapeDtypeStruct((n, D_MODEL), F32), jax.ShapeDtypeStruct((n, D_MODEL), BF16),
                   jax.ShapeDtypeStruct((n, D_MODEL), BF16)],
        compiler_params=pltpu.CompilerParams(
            dimension_semantics=("arbitrary",), vmem_limit_bytes=VMEM_LIMIT),
        name="proj_ffn_conv",
    )(h, mix_a, mix_b, wo, g, w1, w2, cg, cw)


def _conv_ffn_final_kernel(tiles_per_seq, h_ref, bg_ref, t_ref, tp_ref, tn_ref, cw_ref, wo_ref,
                           g_ref, w1_ref, w2_ref, fg_ref, o_ref):
    i = pl.program_id(0)
    t = t_ref[...].astype(F32)
    before = jnp.where(i % tiles_per_seq == 0, 0.0, tp_ref[HALO - 1:HALO, :].astype(F32))
    after = jnp.where(i % tiles_per_seq == tiles_per_seq - 1, 0.0, tn_ref[0:1, :].astype(F32))
    rows_total = t.shape[0]
    r = lax.broadcasted_iota(jnp.int32, t.shape, 0)
    t_prev = jnp.where(r == 0, before, pltpu.roll(t, 1, 0))
    t_next = jnp.where(r == rows_total - 1, after, pltpu.roll(t, rows_total - 1, 0))
    y = t_prev * cw_ref[0:1, :] + t * cw_ref[1:2, :] + t_next * cw_ref[2:3, :]
    for r0 in range(0, rows_total, ROW_TILE):
        rows = slice(r0, r0 + ROW_TILE)
        mix = (bg_ref[rows, :].astype(F32) * y[rows]).astype(BF16)
        h1 = h_ref[rows, :] + _dot(mix, wo_ref[...])
        o_ref[rows, :] = _rms(_ffn_block(h1, g_ref, w1_ref, w2_ref), fg_ref[...])


def _conv_ffn_final(h, bg, t, seq, cw, wo, g, w1, w2, fg):
    n = h.shape[0]
    tile_rows = CONV_CHAINS * ROW_TILE
    tiles_per_seq = seq // tile_rows
    halos_per_tile = tile_rows // HALO
    last_halo = n // HALO - 1
    row = lambda i: (i, 0)
    tile = pl.BlockSpec((tile_rows, D_MODEL), row)
    return pl.pallas_call(
        functools.partial(_conv_ffn_final_kernel, tiles_per_seq),
        grid=(n // tile_rows,),
        in_specs=[
            tile, tile, tile,
            pl.BlockSpec((HALO, D_MODEL), lambda i: (jnp.maximum(i * halos_per_tile - 1, 0), 0)),
            pl.BlockSpec((HALO, D_MODEL),
                         lambda i: (jnp.minimum((i + 1) * halos_per_tile, last_halo), 0)),
            _resident((CONV_W, D_MODEL)),
            _resident((D_MODEL, D_MODEL)),
            _resident((1, D_MODEL)),
            _resident_layer((D_MODEL, D_FF), 1),
            _resident_layer((D_FF, D_MODEL), 1),
            _resident((1, D_MODEL)),
        ],
        out_specs=tile,
        out_shape=jax.ShapeDtypeStruct((n, D_MODEL), F32),
        compiler_params=pltpu.CompilerParams(
            dimension_semantics=("arbitrary",), vmem_limit_bytes=VMEM_LIMIT),
        name="conv_ffn_final",
    )(h, bg, t, t, t, cw, wo, g, w1, w2, fg)


def _rope_tables(seq):
    inv = ROPE_THETA ** (-jnp.arange(0, ROT_DIM, 2, dtype=F32) / ROT_DIM)
    ang = jnp.arange(seq, dtype=F32)[:, None] * inv[None, :]
    cos, sin = jnp.cos(ang), jnp.sin(ang)
    half = ROT_DIM // 2
    lane = jnp.arange(LANES) % B_QK_DIM
    rotated = lane < ROT_DIM
    pick = ((lane % half)[None, :] == jnp.arange(half)[:, None]) & rotated[None, :]
    sign = jnp.where(lane < half, -1.0, 1.0)
    cos_t = jnp.dot(cos, pick.astype(F32), precision=lax.Precision.HIGHEST) + (~rotated).astype(F32)
    sin_t = jnp.dot(sin, pick.astype(F32) * sign, precision=lax.Precision.HIGHEST)
    return cos_t, sin_t


def _trunk(x, p, cos_t, sin_t):
    bsz, seq, _ = x.shape
    h = x.reshape(bsz * seq, D_MODEL)

    out_a, q, k, vt = _mix0_in(h, bsz, seq, p["mix_g"][0], p["e_w_in"], p["a_vnorm_g"], p["a_w_s"],
                               p["a_bias"], cos_t, sin_t)
    lam_init = 0.8 - 0.6 * math.exp(-0.3 * 0)
    out_b = _diff_attn(q, k, vt, p["b_lq1"], p["b_lk1"], p["b_lq2"], p["b_lk2"], p["b_subln_g"],
                       bsz, seq, lam_init)
    h, bg, t = _proj_ffn_conv(h, out_a, out_b, p["e_w_out"], p["ffn_g"][0], p["ffn_w1"],
                              p["ffn_w2"], p["mix_g"][1], p["c_w_in"])
    h = _conv_ffn_final(h, bg, t, seq, p["c_conv_w"], p["c_w_out"], p["ffn_g"][1], p["ffn_w1"],
                        p["ffn_w2"], p["final_g"])
    return h.reshape(bsz, seq, D_MODEL)


def kernel(x_prompt, x_sample, norm_mix_g, norm_ffn_g, ffn_w1, ffn_w2, e_w_in, e_w_out, a_vnorm_g, a_w_s, a_b_s, b_lq1, b_lk1, b_lq2, b_lk2, b_subln_g, c_w_in, c_conv_w, c_w_out, final_g):
    depth = norm_mix_g.shape[0]
    p = {
        "mix_g": [norm_mix_g[i].reshape(1, D_MODEL) for i in range(depth)],
        "ffn_g": [norm_ffn_g[i].reshape(1, D_MODEL) for i in range(depth)],
        "ffn_w1": ffn_w1.astype(BF16),
        "ffn_w2": ffn_w2.astype(BF16),
        "e_w_in": e_w_in[0].astype(BF16),
        "e_w_out": e_w_out[0].astype(BF16),
        "a_vnorm_g": a_vnorm_g[0].reshape(1, A_WIDTH),
        "a_w_s": a_w_s[0].astype(BF16),
        "a_bias": jnp.repeat(a_b_s[0].T, A_GROUP_DIM, axis=1),
        "b_lq1": b_lq1[0].reshape(1, B_QK_DIM),
        "b_lk1": b_lk1[0].reshape(1, B_QK_DIM),
        "b_lq2": b_lq2[0].reshape(1, B_QK_DIM),
        "b_lk2": b_lk2[0].reshape(1, B_QK_DIM),
        "b_subln_g": jnp.broadcast_to(b_subln_g[0][:, None], (B_V_DIM, Q_TILE)),
        "c_w_in": c_w_in[0].astype(BF16),
        "c_conv_w": c_conv_w[0],
        "c_w_out": c_w_out[0].astype(BF16),
        "final_g": final_g.reshape(1, D_MODEL),
    }
    cos_t, sin_t = _rope_tables(max(x_prompt.shape[1], x_sample.shape[1]))
    return (_trunk(x_prompt, p, cos_t, sin_t), _trunk(x_sample, p, cos_t, sin_t))
```

```python
import functools
import math

import jax
import jax.numpy as jnp
from jax import lax
from jax.experimental import pallas as pl
from jax.experimental.pallas import tpu as pltpu

D_MODEL = 1024
D_FF = 4 * D_MODEL
EPS = 1e-5
A_WIDTH = 512
A_GROUPS = 4
A_GROUP_DIM = 128
CHUNK = 128
B_HEADS = 4
B_QK_DIM = 64
B_V_DIM = 128
B_WIDTH = 512
ROT_DIM = 16
ROPE_THETA = 500000.0
E_IN = 2 * A_WIDTH + 2 * B_WIDTH + B_WIDTH
CONV_W = 3

LANES = 128
ROW_TILE = 512
MIX_TILE = 1024
SUB_ROWS = 256
CONV_CHAINS = 2
KV_TILE = 512
UNIT_BLOCKS_PER_STEP = 128
Q_GROUP = 256
SUM_ROWS = 16
HALO = 16
FF_CHUNK = 1024
VMEM_LIMIT = 56 * 1024 * 1024

F32 = jnp.float32
BF16 = jnp.bfloat16


def _resident(shape):
    return pl.BlockSpec(shape, lambda *_: (0,) * len(shape), pipeline_mode=pl.Buffered(1))


def _resident_layer(shape, layer):
    return pl.BlockSpec((None,) + shape, lambda *_: (layer,) + (0,) * len(shape),
                        pipeline_mode=pl.Buffered(1))


def _rms(x, g):
    ms = jnp.mean(x * x, axis=-1, keepdims=True)
    return (x * lax.rsqrt(ms + EPS)) * g


def _dot(a, b):
    return jnp.dot(a, b, preferred_element_type=F32)


def _reduce_rows(x, op, width=32):
    acc = x[0:width]
    for r in range(width, x.shape[0], width):
        acc = op(acc, x[r:r + width])
    while acc.shape[0] > 8:
        half = acc.shape[0] // 2
        acc = op(acc[:half], acc[half:])
    reduce = jnp.max if op is jnp.maximum else jnp.sum
    return reduce(acc, axis=0, keepdims=True)


def _mix0_in_kernel(x_ref, g_ref, w_ref, vg_ref, ws_ref, bias_ref, cos_ref, sin_ref,
                    oa_ref, q_ref, k_ref, vt_ref):
    lane = lax.broadcasted_iota(jnp.int32, (SUB_ROWS, LANES), 1)
    upper_half = (lane % B_QK_DIM) >= ROT_DIM // 2
    scale = math.log2(math.e) / math.sqrt(B_QK_DIM)
    q0, k0, v0 = 2 * A_WIDTH, 2 * A_WIDTH + B_WIDTH, 2 * A_WIDTH + 2 * B_WIDTH

    pieces = []
    for r0 in range(0, MIX_TILE, SUB_ROWS):
        rows = slice(r0, r0 + SUB_ROWS)
        xn = _rms(x_ref[rows, :], g_ref[...]).astype(BF16)
        pieces.append((rows, [_dot(xn, w_ref[:, lo:hi]) for lo, hi in
                              ((A_WIDTH, q0), (0, A_WIDTH), (q0, k0), (k0, v0), (v0, E_IN))]))

    for rows, (av, au, q, k, v) in pieces:
        u = jax.nn.gelu(au)
        gv = jax.nn.gelu(av)
        for g in range(A_GROUPS):
            cols = slice(g * A_GROUP_DIM, (g + 1) * A_GROUP_DIM)
            vn = _rms(gv[:, cols], vg_ref[:, cols]).astype(BF16)
            for c in range(SUB_ROWS // CHUNK):
                sub = slice(c * CHUNK, (c + 1) * CHUNK)
                mixed = _dot(ws_ref[g], vn[sub, :]) + bias_ref[:, cols]
                oa_ref[rows.start + c * CHUNK:rows.start + (c + 1) * CHUNK, cols] = (
                    u[sub, cols] * mixed).astype(BF16)

        cos = cos_ref[rows, :]
        sin = sin_ref[rows, :]

        def rope(t):
            partner = jnp.where(upper_half, pltpu.roll(t, ROT_DIM // 2, 1),
                                pltpu.roll(t, LANES - ROT_DIM // 2, 1))
            return t * cos + partner * sin

        for h in range(B_HEADS):
            cols = slice(h * LANES, (h + 1) * LANES)
            q_ref[rows, cols] = (rope(q[:, cols]) * scale).astype(BF16)
            k_ref[rows, cols] = rope(k[:, cols]).astype(BF16)

        vt = v.T
        for h in range(B_HEADS):
            lanes = slice(rows.start % KV_TILE, rows.start % KV_TILE + SUB_ROWS)
            vt_ref[0, h, rows.start // KV_TILE, :, lanes] = (
                vt[h * B_V_DIM:(h + 1) * B_V_DIM, :].astype(BF16))


def _mix0_in(x2d, bsz, seq, g, w, vg, ws, bias, cos_t, sin_t):
    n = bsz * seq
    tiles_per_seq = seq // MIX_TILE
    assert MIX_TILE % KV_TILE == 0 and KV_TILE % SUB_ROWS == 0
    row = lambda i: (i, 0)
    return pl.pallas_call(
        _mix0_in_kernel,
        grid=(n // MIX_TILE,),
        in_specs=[
            pl.BlockSpec((MIX_TILE, D_MODEL), row),
            _resident((1, D_MODEL)),
            _resident((D_MODEL, E_IN)),
            _resident((1, A_WIDTH)),
            _resident((A_GROUPS, CHUNK, CHUNK)),
            _resident((CHUNK, A_WIDTH)),
            pl.BlockSpec((MIX_TILE, LANES), lambda i: (i % tiles_per_seq, 0)),
            pl.BlockSpec((MIX_TILE, LANES), lambda i: (i % tiles_per_seq, 0)),
        ],
        out_specs=[
            pl.BlockSpec((MIX_TILE, A_WIDTH), row),
            pl.BlockSpec((MIX_TILE, B_WIDTH), row),
            pl.BlockSpec((MIX_TILE, B_WIDTH), row),
            pl.BlockSpec((1, B_HEADS, MIX_TILE // KV_TILE, B_V_DIM, KV_TILE),
                         lambda i: (i // tiles_per_seq, 0, i % tiles_per_seq, 0, 0)),
        ],
        out_shape=[
            jax.ShapeDtypeStruct((n, A_WIDTH), BF16),
            jax.ShapeDtypeStruct((n, B_WIDTH), BF16),
            jax.ShapeDtypeStruct((n, B_WIDTH), BF16),
            jax.ShapeDtypeStruct((bsz, B_HEADS, seq // KV_TILE, B_V_DIM, KV_TILE), BF16),
        ],
        compiler_params=pltpu.CompilerParams(
            dimension_semantics=("arbitrary",), vmem_limit_bytes=VMEM_LIMIT),
        name="mix0_in",
    )(x2d, g, w, vg, ws, bias, cos_t, sin_t)


def _diff_attn_kernel(lam_init, n_kv, q_tiles, q_ref, k_ref, vt_ref, lq1_ref, lk1_ref, lq2_ref,
                      lk2_ref, sg_ref, o_ref, m_ref, acc_ref, s_ref, mb_ref):
    i = pl.program_id(2)
    q_tile = o_ref.shape[0]
    units = [(s, c) for s in range(2) for c in range(q_tile // Q_GROUP)]
    lane = lax.broadcasted_iota(jnp.int32, (Q_GROUP, LANES), 1)
    stream_lanes = (lane < B_QK_DIM, lane >= B_QK_DIM)

    def stage_scores(slot, qt, jb, u):
        s, c = units[u]
        q_row = pl.multiple_of(qt * q_tile + c * Q_GROUP, Q_GROUP)
        qg = q_ref[pl.ds(q_row, Q_GROUP), :]
        qg = jnp.where(stream_lanes[s], qg, jnp.zeros_like(qg))
        kb = k_ref[pl.ds(pl.multiple_of(jb * KV_TILE, KV_TILE), KV_TILE), :]
        st = lax.dot_general(kb, qg, (((1,), (1,)), ((), ())),
                             preferred_element_type=F32)
        s_ref[slot, u] = st
        mb_ref[slot, u] = _reduce_rows(st, jnp.maximum)

    @pl.when(i == 0)
    def _():
        for u in range(len(units)):
            stage_scores(0, 0, 0, u)

    m_ref[...] = jnp.full(m_ref.shape, -jnp.inf, F32)
    acc_ref[...] = jnp.zeros(acc_ref.shape, F32)

    def key_block(jb, slot, prefetch_next):
        vtb = jnp.concatenate([vt_ref[0, 0, jb], jnp.ones((SUM_ROWS, KV_TILE), BF16)], axis=0)
        for u, (s, c) in enumerate(units):
            cols = slice(c * Q_GROUP, (c + 1) * Q_GROUP)
            if prefetch_next:
                stage_scores(1 - slot, i, jb + 1, u)
            m_old = m_ref[s, :, cols]
            m_new = jnp.maximum(m_old, mb_ref[slot, u])
            alpha = jnp.exp2(m_old - m_new)
            p = jnp.exp2(s_ref[slot, u] - m_new)
            acc_ref[s, :, cols] = alpha * acc_ref[s, :, cols] + _dot(vtb, p.astype(BF16))
            m_ref[s, :, cols] = m_new

    for jb in range(n_kv):
        key_block(jb, jb % 2, jb + 1 < n_kv)
    for u in range(len(units)):
        stage_scores(0, jnp.minimum(i + 1, q_tiles - 1), 0, u)

    lam = (jnp.exp(jnp.sum(lq1_ref[...] * lk1_ref[...], axis=-1, keepdims=True))
           - jnp.exp(jnp.sum(lq2_ref[...] * lk2_ref[...], axis=-1, keepdims=True)) + lam_init)
    l1 = acc_ref[0, B_V_DIM:B_V_DIM + 1, :]
    l2 = acc_ref[1, B_V_DIM:B_V_DIM + 1, :]
    ot = acc_ref[0, 0:B_V_DIM, :] * (1.0 / l1) - acc_ref[1, 0:B_V_DIM, :] * (lam / l2)
    ms = jnp.mean(ot * ot, axis=0, keepdims=True)
    y = ((ot * lax.rsqrt(ms + EPS)) * sg_ref[...]) * (1.0 - lam_init)
    o_ref[...] = y.T.astype(BF16)


def _diff_attn(q, k, vt, lq1, lk1, lq2, lk2, sg_col, bsz, seq, lam_init):
    n = bsz * seq
    n_kv = seq // KV_TILE
    q_tile = min(seq, Q_GROUP * UNIT_BLOCKS_PER_STEP // (2 * n_kv))
    assert q_tile % Q_GROUP == 0 and seq % q_tile == 0 and n_kv % 2 == 0
    q_tiles = seq // q_tile
    n_units = 2 * q_tile // Q_GROUP
    sg = jnp.broadcast_to(sg_col, (B_V_DIM, q_tile))
    vec = pl.BlockSpec((1, B_QK_DIM), lambda b, h, i: (0, 0))
    return pl.pallas_call(
        functools.partial(_diff_attn_kernel, lam_init, n_kv, q_tiles),
        grid=(bsz, B_HEADS, q_tiles),
        in_specs=[
            pl.BlockSpec((seq, LANES), lambda b, h, i: (b, h)),
            pl.BlockSpec((seq, LANES), lambda b, h, i: (b, h)),
            pl.BlockSpec((1, 1, n_kv, B_V_DIM, KV_TILE), lambda b, h, i: (b, h, 0, 0, 0)),
            vec, vec, vec, vec,
            pl.BlockSpec((B_V_DIM, q_tile), lambda b, h, i: (0, 0)),
        ],
        out_specs=pl.BlockSpec((q_tile, LANES), lambda b, h, i: (b * q_tiles + i, h)),
        out_shape=jax.ShapeDtypeStruct((n, B_WIDTH), BF16),
        scratch_shapes=[
            pltpu.VMEM((2, 1, q_tile), F32),
            pltpu.VMEM((2, B_V_DIM + SUM_ROWS, q_tile), F32),
            pltpu.VMEM((2, n_units, KV_TILE, Q_GROUP), F32),
            pltpu.VMEM((2, n_units, 1, Q_GROUP), F32),
        ],
        compiler_params=pltpu.CompilerParams(
            dimension_semantics=("arbitrary", "arbitrary", "arbitrary"),
            vmem_limit_bytes=VMEM_LIMIT),
        name="diff_attn",
    )(q, k, vt, lq1, lk1, lq2, lk2, sg)


def _ffn_block(h1, g_ref, w1_ref, w2_ref):
    xn = _rms(h1, g_ref[...]).astype(BF16)
    ff = None
    for c in range(D_FF // FF_CHUNK):
        cols = slice(c * FF_CHUNK, (c + 1) * FF_CHUNK)
        t = jnp.square(jnp.maximum(_dot(xn, w1_ref[:, cols]), 0.0)).astype(BF16)
        d = _dot(t, w2_ref[cols, :])
        ff = d if ff is None else ff + d
    return h1 + ff


def _proj_ffn_conv_kernel(h_ref, ma_ref, mb_ref, wo_ref, g_ref, w1_ref, w2_ref, cg_ref, cw_ref,
                          ho_ref, bg_ref, t_ref):
    half = D_MODEL // 2
    h1 = h_ref[...] + _dot(ma_ref[...], wo_ref[0:half, :]) + _dot(mb_ref[...], wo_ref[half:, :])
    h2 = _ffn_block(h1, g_ref, w1_ref, w2_ref)
    ho_ref[...] = h2
    xn = _rms(h2, cg_ref[...]).astype(BF16)
    bg_ref[...] = _dot(xn, cw_ref[:, 0:D_MODEL]).astype(BF16)
    t_ref[...] = (_dot(xn, cw_ref[:, D_MODEL:2 * D_MODEL])
                  * _dot(xn, cw_ref[:, 2 * D_MODEL:])).astype(BF16)


def _proj_ffn_conv(h, mix_a, mix_b, wo, g, w1, w2, cg, cw):
    n = h.shape[0]
    half = D_MODEL // 2
    row = lambda i: (i, 0)
    tile_f32 = pl.BlockSpec((ROW_TILE, D_MODEL), row)
    return pl.pallas_call(
        _proj_ffn_conv_kernel,
        grid=(n // ROW_TILE,),
        in_specs=[
            tile_f32,
            pl.BlockSpec((ROW_TILE, half), row),
            pl.BlockSpec((ROW_TILE, half), row),
            _resident((D_MODEL, D_MODEL)),
            _resident((1, D_MODEL)),
            _resident_layer((D_MODEL, D_FF), 0),
            _resident_layer((D_FF, D_MODEL), 0),
            _resident((1, D_MODEL)),
            _resident((D_MODEL, 3 * D_MODEL)),
        ],
        out_specs=[tile_f32, pl.BlockSpec((ROW_TILE, D_MODEL), row),
                   pl.BlockSpec((ROW_TILE, D_MODEL), row)],
        out_shape=[jax.ShapeDtypeStruct((n, D_MODEL), F32), jax.ShapeDtypeStruct((n, D_MODEL), BF16),
                   jax.ShapeDtypeStruct((n, D_MODEL), BF16)],
        compiler_params=pltpu.CompilerParams(
            dimension_semantics=("arbitrary",), vmem_limit_bytes=VMEM_LIMIT),
        name="proj_ffn_conv",
    )(h, mix_a, mix_b, wo, g, w1, w2, cg, cw)


def _conv_ffn_final_kernel(tiles_per_seq, h_ref, bg_ref, t_ref, tp_ref, tn_ref, cw_ref, wo_ref,
                           g_ref, w1_ref, w2_ref, fg_ref, o_ref):
    i = pl.program_id(0)
    t = t_ref[...].astype(F32)
    before = jnp.where(i % tiles_per_seq == 0, 0.0, tp_ref[HALO - 1:HALO, :].astype(F32))
    after = jnp.where(i % tiles_per_seq == tiles_per_seq - 1, 0.0, tn_ref[0:1, :].astype(F32))
    rows_total = t.shape[0]
    r = lax.broadcasted_iota(jnp.int32, t.shape, 0)
    t_prev = jnp.where(r == 0, before, pltpu.roll(t, 1, 0))
    t_next = jnp.where(r == rows_total - 1, after, pltpu.roll(t, rows_total - 1, 0))
    y = t_prev * cw_ref[0:1, :] + t * cw_ref[1:2, :] + t_next * cw_ref[2:3, :]
    for r0 in range(0, rows_total, ROW_TILE):
        rows = slice(r0, r0 + ROW_TILE)
        mix = (bg_ref[rows, :].astype(F32) * y[rows]).astype(BF16)
        h1 = h_ref[rows, :] + _dot(mix, wo_ref[...])
        o_ref[rows, :] = _rms(_ffn_block(h1, g_ref, w1_ref, w2_ref), fg_ref[...])


def _conv_ffn_final(h, bg, t, seq, cw, wo, g, w1, w2, fg):
    n = h.shape[0]
    tile_rows = CONV_CHAINS * ROW_TILE
    tiles_per_seq = seq // tile_rows
    halos_per_tile = tile_rows // HALO
    last_halo = n // HALO - 1
    row = lambda i: (i, 0)
    tile = pl.BlockSpec((tile_rows, D_MODEL), row)
    return pl.pallas_call(
        functools.partial(_conv_ffn_final_kernel, tiles_per_seq),
        grid=(n // tile_rows,),
        in_specs=[
            tile, tile, tile,
            pl.BlockSpec((HALO, D_MODEL), lambda i: (jnp.maximum(i * halos_per_tile - 1, 0), 0)),
            pl.BlockSpec((HALO, D_MODEL),
                         lambda i: (jnp.minimum((i + 1) * halos_per_tile, last_halo), 0)),
            _resident((CONV_W, D_MODEL)),
            _resident((D_MODEL, D_MODEL)),
            _resident((1, D_MODEL)),
            _resident_layer((D_MODEL, D_FF), 1),
            _resident_layer((D_FF, D_MODEL), 1),
            _resident((1, D_MODEL)),
        ],
        out_specs=tile,
        out_shape=jax.ShapeDtypeStruct((n, D_MODEL), F32),
        compiler_params=pltpu.CompilerParams(
            dimension_semantics=("arbitrary",), vmem_limit_bytes=VMEM_LIMIT),
        name="conv_ffn_final",
    )(h, bg, t, t, t, cw, wo, g, w1, w2, fg)


def _rope_tables(seq):
    inv = ROPE_THETA ** (-jnp.arange(0, ROT_DIM, 2, dtype=F32) / ROT_DIM)
    ang = jnp.arange(seq, dtype=F32)[:, None] * inv[None, :]
    cos, sin = jnp.cos(ang), jnp.sin(ang)
    half = ROT_DIM // 2
    lane = jnp.arange(LANES) % B_QK_DIM
    rotated = lane < ROT_DIM
    pick = ((lane % half)[None, :] == jnp.arange(half)[:, None]) & rotated[None, :]
    sign = jnp.where(lane < half, -1.0, 1.0)
    cos_t = jnp.dot(cos, pick.astype(F32), precision=lax.Precision.HIGHEST) + (~rotated).astype(F32)
    sin_t = jnp.dot(sin, pick.astype(F32) * sign, precision=lax.Precision.HIGHEST)
    return cos_t, sin_t


def _trunk(x, p, cos_t, sin_t):
    bsz, seq, _ = x.shape
    h = x.reshape(bsz * seq, D_MODEL)

    out_a, q, k, vt = _mix0_in(h, bsz, seq, p["mix_g"][0], p["e_w_in"], p["a_vnorm_g"], p["a_w_s"],
                               p["a_bias"], cos_t, sin_t)
    lam_init = 0.8 - 0.6 * math.exp(-0.3 * 0)
    out_b = _diff_attn(q, k, vt, p["b_lq1"], p["b_lk1"], p["b_lq2"], p["b_lk2"], p["b_subln_g"],
                       bsz, seq, lam_init)
    h, bg, t = _proj_ffn_conv(h, out_a, out_b, p["e_w_out"], p["ffn_g"][0], p["ffn_w1"],
                              p["ffn_w2"], p["mix_g"][1], p["c_w_in"])
    h = _conv_ffn_final(h, bg, t, seq, p["c_conv_w"], p["c_w_out"], p["ffn_g"][1], p["ffn_w1"],
                        p["ffn_w2"], p["final_g"])
    return h.reshape(bsz, seq, D_MODEL)


def kernel(x_prompt, x_sample, norm_mix_g, norm_ffn_g, ffn_w1, ffn_w2, e_w_in, e_w_out, a_vnorm_g, a_w_s, a_b_s, b_lq1, b_lk1, b_lq2, b_lk2, b_subln_g, c_w_in, c_conv_w, c_w_out, final_g):
    depth = norm_mix_g.shape[0]
    p = {
        "mix_g": [norm_mix_g[i].reshape(1, D_MODEL) for i in range(depth)],
        "ffn_g": [norm_ffn_g[i].reshape(1, D_MODEL) for i in range(depth)],
        "ffn_w1": ffn_w1.astype(BF16),
        "ffn_w2": ffn_w2.astype(BF16),
        "e_w_in": e_w_in[0].astype(BF16),
        "e_w_out": e_w_out[0].astype(BF16),
        "a_vnorm_g": a_vnorm_g[0].reshape(1, A_WIDTH),
        "a_w_s": a_w_s[0].astype(BF16),
        "a_bias": jnp.repeat(a_b_s[0].T, A_GROUP_DIM, axis=1),
        "b_lq1": b_lq1[0].reshape(1, B_QK_DIM),
        "b_lk1": b_lk1[0].reshape(1, B_QK_DIM),
        "b_lq2": b_lq2[0].reshape(1, B_QK_DIM),
        "b_lk2": b_lk2[0].reshape(1, B_QK_DIM),
        "b_subln_g": b_subln_g[0].reshape(B_V_DIM, 1),
        "c_w_in": c_w_in[0].astype(BF16),
        "c_conv_w": c_conv_w[0],
        "c_w_out": c_w_out[0].astype(BF16),
        "final_g": final_g.reshape(1, D_MODEL),
    }
    cos_t, sin_t = _rope_tables(max(x_prompt.shape[1], x_sample.shape[1]))
    return (_trunk(x_prompt, p, cos_t, sin_t), _trunk(x_sample, p, cos_t, sin_t))
```

```python
import functools
import math

import jax
import jax.numpy as jnp
from jax import lax
from jax.experimental import pallas as pl
from jax.experimental.pallas import tpu as pltpu

D_MODEL = 1024
D_FF = 4 * D_MODEL
EPS = 1e-5
A_WIDTH = 512
A_GROUPS = 4
A_GROUP_DIM = 128
CHUNK = 128
B_HEADS = 4
B_QK_DIM = 64
B_V_DIM = 128
B_WIDTH = 512
ROT_DIM = 16
ROPE_THETA = 500000.0
E_IN = 2 * A_WIDTH + 2 * B_WIDTH + B_WIDTH
CONV_W = 3

LANES = 128
ROW_TILE = 512
MIX_TILE = 1024
SUB_ROWS = 256
CONV_CHAINS = 2
KV_TILE = 512
UNIT_BLOCKS_PER_STEP = 128
Q_GROUP = 256
SUM_ROWS = 16
HALO = 16
FF_CHUNK = 1024
PROJ_CHAINS = 2
VMEM_LIMIT = 56 * 1024 * 1024
VMEM_LIMIT_LARGE = 62 * 1024 * 1024

F32 = jnp.float32
BF16 = jnp.bfloat16


def _resident(shape):
    return pl.BlockSpec(shape, lambda *_: (0,) * len(shape), pipeline_mode=pl.Buffered(1))


def _resident_layer(shape, layer):
    return pl.BlockSpec((None,) + shape, lambda *_: (layer,) + (0,) * len(shape),
                        pipeline_mode=pl.Buffered(1))


def _rms(x, g):
    ms = jnp.mean(x * x, axis=-1, keepdims=True)
    return (x * lax.rsqrt(ms + EPS)) * g


def _dot(a, b):
    return jnp.dot(a, b, preferred_element_type=F32)


def _reduce_rows(x, op, width=32):
    acc = x[0:width]
    for r in range(width, x.shape[0], width):
        acc = op(acc, x[r:r + width])
    while acc.shape[0] > 8:
        half = acc.shape[0] // 2
        acc = op(acc[:half], acc[half:])
    reduce = jnp.max if op is jnp.maximum else jnp.sum
    return reduce(acc, axis=0, keepdims=True)


def _mix0_in_kernel(x_ref, g_ref, w_ref, vg_ref, ws_ref, bias_ref, cos_ref, sin_ref,
                    oa_ref, q_ref, k_ref, vt_ref):
    lane = lax.broadcasted_iota(jnp.int32, (SUB_ROWS, LANES), 1)
    upper_half = (lane % B_QK_DIM) >= ROT_DIM // 2
    scale = math.log2(math.e) / math.sqrt(B_QK_DIM)
    q0, k0, v0 = 2 * A_WIDTH, 2 * A_WIDTH + B_WIDTH, 2 * A_WIDTH + 2 * B_WIDTH

    pieces = []
    for r0 in range(0, MIX_TILE, SUB_ROWS):
        rows = slice(r0, r0 + SUB_ROWS)
        xn = _rms(x_ref[rows, :], g_ref[...]).astype(BF16)
        pieces.append((rows, [_dot(xn, w_ref[:, lo:hi]) for lo, hi in
                              ((A_WIDTH, q0), (0, A_WIDTH), (q0, k0), (k0, v0), (v0, E_IN))]))

    for rows, (av, au, q, k, v) in pieces:
        u = jax.nn.gelu(au)
        gv = jax.nn.gelu(av)
        for g in range(A_GROUPS):
            cols = slice(g * A_GROUP_DIM, (g + 1) * A_GROUP_DIM)
            vn = _rms(gv[:, cols], vg_ref[:, cols]).astype(BF16)
            for c in range(SUB_ROWS // CHUNK):
                sub = slice(c * CHUNK, (c + 1) * CHUNK)
                mixed = _dot(ws_ref[g], vn[sub, :]) + bias_ref[:, cols]
                oa_ref[rows.start + c * CHUNK:rows.start + (c + 1) * CHUNK, cols] = (
                    u[sub, cols] * mixed).astype(BF16)

        cos = cos_ref[rows, :]
        sin = sin_ref[rows, :]

        def rope(t):
            partner = jnp.where(upper_half, pltpu.roll(t, ROT_DIM // 2, 1),
                                pltpu.roll(t, LANES - ROT_DIM // 2, 1))
            return t * cos + partner * sin

        for h in range(B_HEADS):
            cols = slice(h * LANES, (h + 1) * LANES)
            q_ref[rows, cols] = (rope(q[:, cols]) * scale).astype(BF16)
            k_ref[rows, cols] = rope(k[:, cols]).astype(BF16)

        vt = v.T
        for h in range(B_HEADS):
            lanes = slice(rows.start % KV_TILE, rows.start % KV_TILE + SUB_ROWS)
            vt_ref[0, h, rows.start // KV_TILE, :, lanes] = (
                vt[h * B_V_DIM:(h + 1) * B_V_DIM, :].astype(BF16))


def _mix0_in(x2d, bsz, seq, g, w, vg, ws, bias, cos_t, sin_t):
    n = bsz * seq
    tiles_per_seq = seq // MIX_TILE
    assert MIX_TILE % KV_TILE == 0 and KV_TILE % SUB_ROWS == 0
    row = lambda i: (i, 0)
    return pl.pallas_call(
        _mix0_in_kernel,
        grid=(n // MIX_TILE,),
        in_specs=[
            pl.BlockSpec((MIX_TILE, D_MODEL), row),
            _resident((1, D_MODEL)),
            _resident((D_MODEL, E_IN)),
            _resident((1, A_WIDTH)),
            _resident((A_GROUPS, CHUNK, CHUNK)),
            _resident((CHUNK, A_WIDTH)),
            pl.BlockSpec((MIX_TILE, LANES), lambda i: (i % tiles_per_seq, 0)),
            pl.BlockSpec((MIX_TILE, LANES), lambda i: (i % tiles_per_seq, 0)),
        ],
        out_specs=[
            pl.BlockSpec((MIX_TILE, A_WIDTH), row),
            pl.BlockSpec((MIX_TILE, B_WIDTH), row),
            pl.BlockSpec((MIX_TILE, B_WIDTH), row),
            pl.BlockSpec((1, B_HEADS, MIX_TILE // KV_TILE, B_V_DIM, KV_TILE),
                         lambda i: (i // tiles_per_seq, 0, i % tiles_per_seq, 0, 0)),
        ],
        out_shape=[
            jax.ShapeDtypeStruct((n, A_WIDTH), BF16),
            jax.ShapeDtypeStruct((n, B_WIDTH), BF16),
            jax.ShapeDtypeStruct((n, B_WIDTH), BF16),
            jax.ShapeDtypeStruct((bsz, B_HEADS, seq // KV_TILE, B_V_DIM, KV_TILE), BF16),
        ],
        compiler_params=pltpu.CompilerParams(
            dimension_semantics=("arbitrary",), vmem_limit_bytes=VMEM_LIMIT),
        name="mix0_in",
    )(x2d, g, w, vg, ws, bias, cos_t, sin_t)


def _diff_attn_kernel(lam_init, n_kv, q_tiles, q_ref, k_ref, vt_ref, lq1_ref, lk1_ref, lq2_ref,
                      lk2_ref, sg_ref, o_ref, m_ref, acc_ref, s_ref, mb_ref):
    i = pl.program_id(2)
    q_tile = o_ref.shape[0]
    units = [(s, c) for s in range(2) for c in range(q_tile // Q_GROUP)]
    lane = lax.broadcasted_iota(jnp.int32, (Q_GROUP, LANES), 1)
    stream_lanes = (lane < B_QK_DIM, lane >= B_QK_DIM)

    def stage_scores(slot, qt, jb, u):
        s, c = units[u]
        q_row = pl.multiple_of(qt * q_tile + c * Q_GROUP, Q_GROUP)
        qg = q_ref[pl.ds(q_row, Q_GROUP), :]
        qg = jnp.where(stream_lanes[s], qg, jnp.zeros_like(qg))
        kb = k_ref[pl.ds(pl.multiple_of(jb * KV_TILE, KV_TILE), KV_TILE), :]
        st = lax.dot_general(kb, qg, (((1,), (1,)), ((), ())),
                             preferred_element_type=F32)
        s_ref[slot, u] = st
        mb_ref[slot, u] = _reduce_rows(st, jnp.maximum)

    @pl.when(i == 0)
    def _():
        for u in range(len(units)):
            stage_scores(0, 0, 0, u)

    m_ref[...] = jnp.full(m_ref.shape, -jnp.inf, F32)
    acc_ref[...] = jnp.zeros(acc_ref.shape, F32)

    def key_block(jb, slot, prefetch_next):
        vtb = jnp.concatenate([vt_ref[0, 0, jb], jnp.ones((SUM_ROWS, KV_TILE), BF16)], axis=0)
        for u, (s, c) in enumerate(units):
            cols = slice(c * Q_GROUP, (c + 1) * Q_GROUP)
            if prefetch_next:
                stage_scores(1 - slot, i, jb + 1, u)
            m_old = m_ref[s, :, cols]
            m_new = jnp.maximum(m_old, mb_ref[slot, u])
            alpha = jnp.exp2(m_old - m_new)
            p = jnp.exp2(s_ref[slot, u] - m_new)
            acc_ref[s, :, cols] = alpha * acc_ref[s, :, cols] + _dot(vtb, p.astype(BF16))
            m_ref[s, :, cols] = m_new

    for jb in range(n_kv):
        key_block(jb, jb % 2, jb + 1 < n_kv)
    for u in range(len(units)):
        stage_scores(0, jnp.minimum(i + 1, q_tiles - 1), 0, u)

    lam = (jnp.exp(jnp.sum(lq1_ref[...] * lk1_ref[...], axis=-1, keepdims=True))
           - jnp.exp(jnp.sum(lq2_ref[...] * lk2_ref[...], axis=-1, keepdims=True)) + lam_init)
    l1 = acc_ref[0, B_V_DIM:B_V_DIM + 1, :]
    l2 = acc_ref[1, B_V_DIM:B_V_DIM + 1, :]
    ot = acc_ref[0, 0:B_V_DIM, :] * (1.0 / l1) - acc_ref[1, 0:B_V_DIM, :] * (lam / l2)
    ms = jnp.mean(ot * ot, axis=0, keepdims=True)
    y = ((ot * lax.rsqrt(ms + EPS)) * sg_ref[...]) * (1.0 - lam_init)
    o_ref[...] = y.T.astype(BF16)


def _diff_attn(q, k, vt, lq1, lk1, lq2, lk2, sg_col, bsz, seq, lam_init):
    n = bsz * seq
    n_kv = seq // KV_TILE
    q_tile = min(seq, Q_GROUP * UNIT_BLOCKS_PER_STEP // (2 * n_kv))
    assert q_tile % Q_GROUP == 0 and seq % q_tile == 0 and n_kv % 2 == 0
    q_tiles = seq // q_tile
    n_units = 2 * q_tile // Q_GROUP
    sg = jnp.broadcast_to(sg_col, (B_V_DIM, q_tile))
    vec = pl.BlockSpec((1, B_QK_DIM), lambda b, h, i: (0, 0))
    return pl.pallas_call(
        functools.partial(_diff_attn_kernel, lam_init, n_kv, q_tiles),
        grid=(bsz, B_HEADS, q_tiles),
        in_specs=[
            pl.BlockSpec((seq, LANES), lambda b, h, i: (b, h)),
            pl.BlockSpec((seq, LANES), lambda b, h, i: (b, h)),
            pl.BlockSpec((1, 1, n_kv, B_V_DIM, KV_TILE), lambda b, h, i: (b, h, 0, 0, 0)),
            vec, vec, vec, vec,
            pl.BlockSpec((B_V_DIM, q_tile), lambda b, h, i: (0, 0)),
        ],
        out_specs=pl.BlockSpec((q_tile, LANES), lambda b, h, i: (b * q_tiles + i, h)),
        out_shape=jax.ShapeDtypeStruct((n, B_WIDTH), BF16),
        scratch_shapes=[
            pltpu.VMEM((2, 1, q_tile), F32),
            pltpu.VMEM((2, B_V_DIM + SUM_ROWS, q_tile), F32),
            pltpu.VMEM((2, n_units, KV_TILE, Q_GROUP), F32),
            pltpu.VMEM((2, n_units, 1, Q_GROUP), F32),
        ],
        compiler_params=pltpu.CompilerParams(
            dimension_semantics=("arbitrary", "arbitrary", "arbitrary"),
            vmem_limit_bytes=VMEM_LIMIT),
        name="diff_attn",
    )(q, k, vt, lq1, lk1, lq2, lk2, sg)


def _ffn_block(h1, g_ref, w1_ref, w2_ref):
    xn = _rms(h1, g_ref[...]).astype(BF16)
    ff = None
    for c in range(D_FF // FF_CHUNK):
        cols = slice(c * FF_CHUNK, (c + 1) * FF_CHUNK)
        t = jnp.square(jnp.maximum(_dot(xn, w1_ref[:, cols]), 0.0)).astype(BF16)
        d = _dot(t, w2_ref[cols, :])
        ff = d if ff is None else ff + d
    return h1 + ff


def _proj_ffn_conv_kernel(h_ref, ma_ref, mb_ref, wo_ref, g_ref, w1_ref, w2_ref, cg_ref, cw_ref,
                          ho_ref, bg_ref, t_ref):
    half = D_MODEL // 2
    for r0 in range(0, h_ref.shape[0], ROW_TILE):
        rows = slice(r0, r0 + ROW_TILE)
        h1 = (h_ref[rows, :] + _dot(ma_ref[rows, :], wo_ref[0:half, :])
              + _dot(mb_ref[rows, :], wo_ref[half:, :]))
        h2 = _ffn_block(h1, g_ref, w1_ref, w2_ref)
        ho_ref[rows, :] = h2
        xn = _rms(h2, cg_ref[...]).astype(BF16)
        bg_ref[rows, :] = _dot(xn, cw_ref[:, 0:D_MODEL]).astype(BF16)
        t_ref[rows, :] = (_dot(xn, cw_ref[:, D_MODEL:2 * D_MODEL])
                          * _dot(xn, cw_ref[:, 2 * D_MODEL:])).astype(BF16)


def _proj_ffn_conv(h, mix_a, mix_b, wo, g, w1, w2, cg, cw):
    n = h.shape[0]
    half = D_MODEL // 2
    row = lambda i: (i, 0)
    tile_rows = PROJ_CHAINS * ROW_TILE
    tile_f32 = pl.BlockSpec((tile_rows, D_MODEL), row)
    return pl.pallas_call(
        _proj_ffn_conv_kernel,
        grid=(n // tile_rows,),
        in_specs=[
            tile_f32,
            pl.BlockSpec((tile_rows, half), row),
            pl.BlockSpec((tile_rows, half), row),
            _resident((D_MODEL, D_MODEL)),
            _resident((1, D_MODEL)),
            _resident_layer((D_MODEL, D_FF), 0),
            _resident_layer((D_FF, D_MODEL), 0),
            _resident((1, D_MODEL)),
            _resident((D_MODEL, 3 * D_MODEL)),
        ],
        out_specs=[tile_f32, pl.BlockSpec((tile_rows, D_MODEL), row),
                   pl.BlockSpec((tile_rows, D_MODEL), row)],
        out_shape=[jax.ShapeDtypeStruct((n, D_MODEL), F32), jax.ShapeDtypeStruct((n, D_MODEL), BF16),
                   jax.ShapeDtypeStruct((n, D_MODEL), BF16)],
        compiler_params=pltpu.CompilerParams(
            dimension_semantics=("arbitrary",), vmem_limit_bytes=VMEM_LIMIT_LARGE),
        name="proj_ffn_conv",
    )(h, mix_a, mix_b, wo, g, w1, w2, cg, cw)


def _conv_ffn_final_kernel(tiles_per_seq, h_ref, bg_ref, t_ref, tp_ref, tn_ref, cw_ref, wo_ref,
                           g_ref, w1_ref, w2_ref, fg_ref, o_ref):
    i = pl.program_id(0)
    t = t_ref[...].astype(F32)
    before = jnp.where(i % tiles_per_seq == 0, 0.0, tp_ref[HALO - 1:HALO, :].astype(F32))
    after = jnp.where(i % tiles_per_seq == tiles_per_seq - 1, 0.0, tn_ref[0:1, :].astype(F32))
    rows_total = t.shape[0]
    r = lax.broadcasted_iota(jnp.int32, t.shape, 0)
    t_prev = jnp.where(r == 0, before, pltpu.roll(t, 1, 0))
    t_next = jnp.where(r == rows_total - 1, after, pltpu.roll(t, rows_total - 1, 0))
    y = t_prev * cw_ref[0:1, :] + t * cw_ref[1:2, :] + t_next * cw_ref[2:3, :]
    for r0 in range(0, rows_total, ROW_TILE):
        rows = slice(r0, r0 + ROW_TILE)
        mix = (bg_ref[rows, :].astype(F32) * y[rows]).astype(BF16)
        h1 = h_ref[rows, :] + _dot(mix, wo_ref[...])
        o_ref[rows, :] = _rms(_ffn_block(h1, g_ref, w1_ref, w2_ref), fg_ref[...])


def _conv_ffn_final(h, bg, t, seq, cw, wo, g, w1, w2, fg):
    n = h.shape[0]
    tile_rows = CONV_CHAINS * ROW_TILE
    tiles_per_seq = seq // tile_rows
    halos_per_tile = tile_rows // HALO
    last_halo = n // HALO - 1
    row = lambda i: (i, 0)
    tile = pl.BlockSpec((tile_rows, D_MODEL), row)
    return pl.pallas_call(
        functools.partial(_conv_ffn_final_kernel, tiles_per_seq),
        grid=(n // tile_rows,),
        in_specs=[
            tile, tile, tile,
            pl.BlockSpec((HALO, D_MODEL), lambda i: (jnp.maximum(i * halos_per_tile - 1, 0), 0)),
            pl.BlockSpec((HALO, D_MODEL),
                         lambda i: (jnp.minimum((i + 1) * halos_per_tile, last_halo), 0)),
            _resident((CONV_W, D_MODEL)),
            _resident((D_MODEL, D_MODEL)),
            _resident((1, D_MODEL)),
            _resident_layer((D_MODEL, D_FF), 1),
            _resident_layer((D_FF, D_MODEL), 1),
            _resident((1, D_MODEL)),
        ],
        out_specs=tile,
        out_shape=jax.ShapeDtypeStruct((n, D_MODEL), F32),
        compiler_params=pltpu.CompilerParams(
            dimension_semantics=("arbitrary",), vmem_limit_bytes=VMEM_LIMIT),
        name="conv_ffn_final",
    )(h, bg, t, t, t, cw, wo, g, w1, w2, fg)


def _rope_tables(seq):
    inv = ROPE_THETA ** (-jnp.arange(0, ROT_DIM, 2, dtype=F32) / ROT_DIM)
    ang = jnp.arange(seq, dtype=F32)[:, None] * inv[None, :]
    cos, sin = jnp.cos(ang), jnp.sin(ang)
    half = ROT_DIM // 2
    lane = jnp.arange(LANES) % B_QK_DIM
    rotated = lane < ROT_DIM
    pick = ((lane % half)[None, :] == jnp.arange(half)[:, None]) & rotated[None, :]
    sign = jnp.where(lane < half, -1.0, 1.0)
    cos_t = jnp.dot(cos, pick.astype(F32), precision=lax.Precision.HIGHEST) + (~rotated).astype(F32)
    sin_t = jnp.dot(sin, pick.astype(F32) * sign, precision=lax.Precision.HIGHEST)
    return cos_t, sin_t


def _trunk(x, p, cos_t, sin_t):
    bsz, seq, _ = x.shape
    h = x.reshape(bsz * seq, D_MODEL)

    out_a, q, k, vt = _mix0_in(h, bsz, seq, p["mix_g"][0], p["e_w_in"], p["a_vnorm_g"], p["a_w_s"],
                               p["a_bias"], cos_t, sin_t)
    lam_init = 0.8 - 0.6 * math.exp(-0.3 * 0)
    out_b = _diff_attn(q, k, vt, p["b_lq1"], p["b_lk1"], p["b_lq2"], p["b_lk2"], p["b_subln_g"],
                       bsz, seq, lam_init)
    h, bg, t = _proj_ffn_conv(h, out_a, out_b, p["e_w_out"], p["ffn_g"][0], p["ffn_w1"],
                              p["ffn_w2"], p["mix_g"][1], p["c_w_in"])
    h = _conv_ffn_final(h, bg, t, seq, p["c_conv_w"], p["c_w_out"], p["ffn_g"][1], p["ffn_w1"],
                        p["ffn_w2"], p["final_g"])
    return h.reshape(bsz, seq, D_MODEL)


def kernel(x_prompt, x_sample, norm_mix_g, norm_ffn_g, ffn_w1, ffn_w2, e_w_in, e_w_out, a_vnorm_g, a_w_s, a_b_s, b_lq1, b_lk1, b_lq2, b_lk2, b_subln_g, c_w_in, c_conv_w, c_w_out, final_g):
    depth = norm_mix_g.shape[0]
    p = {
        "mix_g": [norm_mix_g[i].reshape(1, D_MODEL) for i in range(depth)],
        "ffn_g": [norm_ffn_g[i].reshape(1, D_MODEL) for i in range(depth)],
        "ffn_w1": ffn_w1.astype(BF16),
        "ffn_w2": ffn_w2.astype(BF16),
        "e_w_in": e_w_in[0].astype(BF16),
        "e_w_out": e_w_out[0].astype(BF16),
        "a_vnorm_g": a_vnorm_g[0].reshape(1, A_WIDTH),
        "a_w_s": a_w_s[0].astype(BF16),
        "a_bias": jnp.repeat(a_b_s[0].T, A_GROUP_DIM, axis=1),
        "b_lq1": b_lq1[0].reshape(1, B_QK_DIM),
        "b_lk1": b_lk1[0].reshape(1, B_QK_DIM),
        "b_lq2": b_lq2[0].reshape(1, B_QK_DIM),
        "b_lk2": b_lk2[0].reshape(1, B_QK_DIM),
        "b_subln_g": b_subln_g[0].reshape(B_V_DIM, 1),
        "c_w_in": c_w_in[0].astype(BF16),
        "c_conv_w": c_conv_w[0],
        "c_w_out": c_w_out[0].astype(BF16),
        "final_g": final_g.reshape(1, D_MODEL),
    }
    cos_t, sin_t = _rope_tables(max(x_prompt.shape[1], x_sample.shape[1]))
    return (_trunk(x_prompt, p, cos_t, sin_t), _trunk(x_sample, p, cos_t, sin_t))
```
